```python
import math
import jax, jax.numpy as jnp
from jax import lax
import numpy as np

D_MODEL = 2048
BATCH = 1
SEQ = 16384
DEPTH = 4

GRID_W = 64
CTX_LEN = 256
Q_BLOCK = 128
RET_CHUNK = 128
ROPE_THETA = 10000.0
EPS = 1e-6

A_WIDTH = D_MODEL // 2
A_HEADS = 8
A_HEAD_DIM = A_WIDTH // A_HEADS // 2
A_V_DIM = 2 * A_HEAD_DIM
A_QK_WIDTH = 2 * A_HEADS * A_HEAD_DIM
B_WIDTH = D_MODEL - A_WIDTH
B_HEADS = 4
B_V_DIM = B_WIDTH // B_HEADS
B_K_DIM = B_V_DIM // 2
B_QK_WIDTH = B_HEADS * B_K_DIM
EV_SIZES = (A_QK_WIDTH, A_QK_WIDTH, A_WIDTH, A_WIDTH, B_QK_WIDTH, B_QK_WIDTH, B_WIDTH, B_WIDTH)
EV_IN = sum(EV_SIZES)
EV_MIX = A_WIDTH + B_WIDTH

C_HEADS = 16
C_KV_HEADS = 4
C_GROUP = C_HEADS // C_KV_HEADS
C_HEAD_DIM = D_MODEL // C_HEADS
C_WIDTH = C_HEADS * C_HEAD_DIM
C_KV_WIDTH = C_KV_HEADS * C_HEAD_DIM
OD_SIZES = (C_WIDTH, C_KV_WIDTH, C_KV_WIDTH, C_WIDTH)
OD_IN = sum(OD_SIZES)

N_EVEN = (DEPTH + 1) // 2
N_ODD = DEPTH // 2

kernel_name = 'hybrid_diffattn_retention_gqa_dit'


def rms_norm(x, g):
    x32 = x.astype(jnp.float32)
    y = x32 * lax.rsqrt(jnp.mean(x32 * x32, axis=-1, keepdims=True) + EPS)
    return y.astype(x.dtype) * g


def group_norm(x, g):
    x32 = x.astype(jnp.float32)
    mu = jnp.mean(x32, axis=-1, keepdims=True)
    xc = x32 - mu
    y = xc * lax.rsqrt(jnp.mean(xc * xc, axis=-1, keepdims=True) + EPS)
    return y.astype(x.dtype) * g


def split_sizes(t, sizes):
    idx, acc = [], 0
    for s in sizes[:-1]:
        acc += s
        idx.append(acc)
    return jnp.split(t, idx, axis=-1)


def to_heads(t, n_heads):
    b, l, _ = t.shape
    return t.reshape(b, l, n_heads, -1).transpose(0, 2, 1, 3)


def merge_heads(t):
    b, h, l, d = t.shape
    return t.transpose(0, 2, 1, 3).reshape(b, l, h * d)


def grid_positions(n_lat):
    n_rows = n_lat // GRID_W
    rows = jnp.broadcast_to(jnp.arange(n_rows, dtype=jnp.int32)[:, None], (n_rows, GRID_W)).reshape(-1)
    cols = jnp.broadcast_to(jnp.arange(GRID_W, dtype=jnp.int32)[None, :], (n_rows, GRID_W)).reshape(-1)
    return rows, cols


def rope_tables(rows, cols, head_dim):
    axis_dim = head_dim // 2
    inv = ROPE_THETA ** (-jnp.arange(0, axis_dim, 2, dtype=jnp.float32) / axis_dim)
    ang_r = rows.astype(jnp.float32)[:, None] * inv
    ang_c = cols.astype(jnp.float32)[:, None] * inv
    return (jnp.cos(ang_r), jnp.sin(ang_r), jnp.cos(ang_c), jnp.sin(ang_c))


def _rotate(x, cos, sin):
    x1, x2 = jnp.split(x, 2, axis=-1)
    return jnp.concatenate([x1 * cos - x2 * sin, x1 * sin + x2 * cos], axis=-1)


def apply_rope_2d(x, tables):
    cr, sr, cc, sc = tables
    xr, xc = jnp.split(x, 2, axis=-1)
    return jnp.concatenate([_rotate(xr, cr, sr), _rotate(xc, cc, sc)], axis=-1).astype(x.dtype)


def _to_blocks(q):
    *lead, n, d = q.shape
    return jnp.moveaxis(q.reshape(*lead, n // Q_BLOCK, Q_BLOCK, d), -3, 0)


def _from_blocks(o):
    o = jnp.moveaxis(o, 0, -3)
    *lead, nb, qb, d = o.shape
    return o.reshape(*lead, nb * qb, d)


def diff_attn_core(q1, q2, k1, k2, v, lam):
    scale = A_HEAD_DIM ** -0.5
    s1 = jnp.einsum('bhqd,bhkd->bhqk', q1, k1) * scale
    s2 = jnp.einsum('bhqd,bhkd->bhqk', q2, k2) * scale
    p = jax.nn.softmax(s1.astype(jnp.float32), axis=-1) - lam * jax.nn.softmax(s2.astype(jnp.float32), axis=-1)
    return jnp.einsum('bhqk,bhke->bhqe', p.astype(v.dtype), v)


def gqa_core(q, k, v):
    s = jnp.einsum('bhgqd,bhkd->bhgqk', q, k) * (C_HEAD_DIM ** -0.5)
    p = jax.nn.softmax(s.astype(jnp.float32), axis=-1).astype(v.dtype)
    return jnp.einsum('bhgqk,bhkd->bhgqd', p, v)


def retention_chunkwise(q, k, v, log_g, s0):
    b, h, l, dk = q.shape
    dv = v.shape[-1]
    nc = l // RET_CHUNK
    dt = q.dtype
    qc = q.reshape(b, h, nc, RET_CHUNK, dk)
    kc = k.reshape(b, h, nc, RET_CHUNK, dk)
    vc = v.reshape(b, h, nc, RET_CHUNK, dv)
    pos = jnp.arange(RET_CHUNK, dtype=jnp.float32)
    lg = log_g.astype(jnp.float32)[:, None]
    rel = pos[:, None] - pos[None, :]
    decay = jnp.where(rel >= 0, jnp.exp(lg[:, :, None] * jnp.maximum(rel, 0.0)), 0.0)
    q_decay = jnp.exp(lg * (pos + 1.0))
    k_decay = jnp.exp(lg * (RET_CHUNK - 1.0 - pos))
    chunk_decay = jnp.exp(lg[:, 0] * RET_CHUNK).astype(dt)
    scores = jnp.einsum('bhnid,bhnjd->bhnij', qc, kc) * decay[None, :, None].astype(dt)
    intra = jnp.einsum('bhnij,bhnje->bhnie', scores, vc)
    upd = jnp.einsum('bhnjd,bhnje->nbhde', kc * k_decay[None, :, None, :, None].astype(dt), vc)

    def step(s, u):
        return s * chunk_decay[None, :, None, None] + u, s

    s_final, s_prev = lax.scan(step, s0, upd)
    cross = jnp.einsum('bhnid,nbhde->bhnie', qc * q_decay[None, :, None, :, None].astype(dt), s_prev)
    return (intra + cross).reshape(b, h, l, dv), s_final


def bi_retention(q_c, k_c, v_c, q_l, k_l, v_l, log_g_fwd, log_g_bwd):
    flip = lambda t: jnp.flip(t, axis=2)
    b, h, _, dk = q_c.shape
    s0 = jnp.zeros((b, h, dk, v_c.shape[-1]), q_c.dtype)
    o_cf, s_cf = retention_chunkwise(q_c, k_c, v_c, log_g_fwd, s0)
    o_cb, s_cb = retention_chunkwise(flip(q_c), flip(k_c), flip(v_c), log_g_bwd, s0)
    o_lf, _ = retention_chunkwise(q_l, k_l, v_l, log_g_fwd, s_cf)
    o_lb, _ = retention_chunkwise(flip(q_l), flip(k_l), flip(v_l), log_g_bwd, s_cb)
    return o_lf + flip(o_lb), o_cf + flip(o_cb)


def even_mixer(h_lat, h_ctx, w_in, w_out, lam_p, subln_g, ret_log_a, ret_g, rope_a, layer_idx, need_ctx):
    lambda_init = 0.8 - 0.6 * math.exp(-0.3 * layer_idx)
    lp = lam_p.astype(jnp.float32)
    lam = jnp.exp(jnp.sum(lp[0] * lp[1])) - jnp.exp(jnp.sum(lp[2] * lp[3])) + lambda_init
    log_g = -jnp.exp(ret_log_a.astype(jnp.float32))

    def project(h, rope):
        aq, ak, av, ag, bq, bk, bv, bg = split_sizes(h @ w_in, EV_SIZES)
        aq = to_heads(aq, 2 * A_HEADS)
        ak = to_heads(ak, 2 * A_HEADS)
        if rope is not None:
            aq = apply_rope_2d(aq, rope)
            ak = apply_rope_2d(ak, rope)
        b, _, l, _ = aq.shape
        aq = aq.reshape(b, A_HEADS, 2, l, A_HEAD_DIM)
        ak = ak.reshape(b, A_HEADS, 2, l, A_HEAD_DIM)
        return (aq[:, :, 0], aq[:, :, 1], ak[:, :, 0], ak[:, :, 1], to_heads(av, A_HEADS), ag,
                to_heads(bq, B_HEADS), to_heads(bk, B_HEADS) * (B_K_DIM ** -0.5), to_heads(bv, B_HEADS), bg)

    def finish(o_a, a_gate, o_b, b_gate):
        ya = merge_heads(rms_norm(o_a, subln_g) * (1.0 - lambda_init)) * jax.nn.silu(a_gate)
        yb = merge_heads(group_norm(o_b, ret_g[None, :, None, :])) * jax.nn.silu(b_gate)
        return jnp.concatenate([ya, yb], axis=-1) @ w_out

    lat = project(h_lat, rope_a)
    cx = project(h_ctx, None)
    k1_all = jnp.concatenate([cx[2], lat[2]], axis=2)
    k2_all = jnp.concatenate([cx[3], lat[3]], axis=2)
    v_all = jnp.concatenate([cx[4], lat[4]], axis=2)
    o_a_lat = _from_blocks(lax.map(lambda qs: diff_attn_core(qs[0], qs[1], k1_all, k2_all, v_all, lam),
                                   (_to_blocks(lat[0]), _to_blocks(lat[1]))))
    o_b_lat, o_b_ctx = bi_retention(cx[6], cx[7], cx[8], lat[6], lat[7], lat[8], log_g[0], log_g[1])
    y_lat = finish(o_a_lat, lat[5], o_b_lat, lat[9])
    y_ctx = None
    if need_ctx:
        o_a_ctx = diff_attn_core(cx[0], cx[1], cx[2], cx[3], cx[4], lam)
        y_ctx = finish(o_a_ctx, cx[5], o_b_ctx, cx[9])
    return y_lat, y_ctx


def odd_mixer(h_lat, h_ctx, w_in, w_out, qk_g, rope_c, need_ctx):
    def project(h, rope):
        q, k, v, g = split_sizes(h @ w_in, OD_SIZES)
        q = rms_norm(to_heads(q, C_HEADS), qk_g[0])
        k = rms_norm(to_heads(k, C_KV_HEADS), qk_g[1])
        if rope is not None:
            q = apply_rope_2d(q, rope)
            k = apply_rope_2d(k, rope)
        b, _, l, _ = q.shape
        return q.reshape(b, C_KV_HEADS, C_GROUP, l, C_HEAD_DIM), k, to_heads(v, C_KV_HEADS), g

    def finish(o, g):
        b, hk, gr, l, d = o.shape
        return (merge_heads(o.reshape(b, hk * gr, l, d)) * jax.nn.silu(g)) @ w_out

    ql, kl, vl, gl = project(h_lat, rope_c)
    qc, kc, vc, gc = project(h_ctx, None)
    k_all = jnp.concatenate([kc, kl], axis=2)
    v_all = jnp.concatenate([vc, vl], axis=2)
    o_lat = _from_blocks(lax.map(lambda qb: gqa_core(qb, k_all, v_all), _to_blocks(ql)))
    y_lat = finish(o_lat, gl)
    y_ctx = finish(gqa_core(qc, kc, vc), gc) if need_ctx else None
    return y_lat, y_ctx


def setup_inputs(seed: int = 0) -> dict:
    key = jax.random.key(seed)
    ks = jax.random.split(key, 20)
    f32 = jnp.float32
    nrm = lambda k, shape, s: jax.random.normal(k, shape, f32) * s
    heads = jnp.arange(B_HEADS, dtype=f32)
    base_decay = jnp.log(-jnp.log1p(-(2.0 ** (-5.0 - heads))))
    return {
        'x': nrm(ks[0], (BATCH, SEQ, D_MODEL), 1.0),
        'c': nrm(ks[1], (BATCH, D_MODEL), 1.0),
        'ctx': nrm(ks[2], (BATCH, CTX_LEN, D_MODEL), 1.0),
        'c_ctx': nrm(ks[3], (D_MODEL,), 1.0),
        'norm_g': 1.0 + nrm(ks[4], (DEPTH, D_MODEL), 0.02),
        'w_mod': nrm(ks[5], (DEPTH, D_MODEL, 3 * D_MODEL), D_MODEL ** -0.5),
        'b_mod': nrm(ks[6], (DEPTH, 3 * D_MODEL), 0.02),
        'ev_w_in': nrm(ks[7], (N_EVEN, D_MODEL, EV_IN), D_MODEL ** -0.5),
        'ev_w_out': nrm(ks[8], (N_EVEN, EV_MIX, D_MODEL), EV_MIX ** -0.5),
        'diff_lambda': nrm(ks[9], (N_EVEN, 4, A_HEAD_DIM), 0.1),
        'diff_subln': 1.0 + nrm(ks[10], (N_EVEN, A_V_DIM), 0.02),
        'ret_decay': base_decay + nrm(ks[11], (N_EVEN, 2, B_HEADS), 0.05),
        'ret_gn': 1.0 + nrm(ks[12], (N_EVEN, B_HEADS, B_V_DIM), 0.02),
        'od_w_in': nrm(ks[13], (N_ODD, D_MODEL, OD_IN), D_MODEL ** -0.5),
        'od_w_out': nrm(ks[14], (N_ODD, C_WIDTH, D_MODEL), C_WIDTH ** -0.5),
        'qk_norm': 1.0 + nrm(ks[15], (N_ODD, 2, C_HEAD_DIM), 0.02),
        'final_norm': 1.0 + nrm(ks[16], (D_MODEL,), 0.02),
    }


def reference(x, c, ctx, c_ctx, norm_g, w_mod, b_mod, ev_w_in, ev_w_out, diff_lambda, diff_subln,
              ret_decay, ret_gn, od_w_in, od_w_out, qk_norm, final_norm):
    n_lat = x.shape[1]
    rows, cols = grid_positions(n_lat)
    rope_a = rope_tables(rows, cols, A_HEAD_DIM)
    rope_c = rope_tables(rows, cols, C_HEAD_DIM)
    s_lat = jax.nn.silu(c)
    s_ctx = jax.nn.silu(c_ctx)
    for i in range(DEPTH):
        shift, scale, gate = jnp.split(s_lat @ w_mod[i] + b_mod[i], 3, axis=-1)
        shift_c, scale_c, gate_c = jnp.split(s_ctx @ w_mod[i] + b_mod[i], 3, axis=-1)
        h_lat = rms_norm(x, norm_g[i]) * (1 + scale[:, None]) + shift[:, None]
        h_ctx = rms_norm(ctx, norm_g[i]) * (1 + scale_c) + shift_c
        need_ctx = i < DEPTH - 1
        if i % 2 == 0:
            e = i // 2
            y_lat, y_ctx = even_mixer(h_lat, h_ctx, ev_w_in[e], ev_w_out[e], diff_lambda[e], diff_subln[e],
                                      ret_decay[e], ret_gn[e], rope_a, i, need_ctx)
        else:
            o = i // 2
            y_lat, y_ctx = odd_mixer(h_lat, h_ctx, od_w_in[o], od_w_out[o], qk_norm[o], rope_c, need_ctx)
        x = x + gate[:, None] * y_lat
        if need_ctx:
            ctx = ctx + gate_c * y_ctx
    return rms_norm(x, final_norm)
```

```python
import functools
import math

import jax
import jax.numpy as jnp
from jax import lax
from jax.experimental import pallas as pl
from jax.experimental.pallas import tpu as pltpu

F32 = jnp.float32
BF16 = jnp.bfloat16

D_MODEL = 2048
DEPTH = 4
GRID_W = 64
CTX_LEN = 256
RET_CHUNK = 128
ROPE_THETA = 10000.0
EPS = 1e-6

A_WIDTH = D_MODEL // 2
A_HEADS = 8
A_HEAD_DIM = A_WIDTH // A_HEADS // 2
A_V_DIM = 2 * A_HEAD_DIM
B_WIDTH = D_MODEL - A_WIDTH
B_HEADS = 4
B_V_DIM = B_WIDTH // B_HEADS
B_K_DIM = B_V_DIM // 2
EV_IN = 7168

C_HEADS = 16
C_KV_HEADS = 4
C_GROUP = C_HEADS // C_KV_HEADS
C_HEAD_DIM = D_MODEL // C_HEADS
OD_IN = 5120

LANES = 128
VMEM_LIMIT_BYTES = 56 * 1024 * 1024
PROJ_TM = 1280
PROJ_TN = 512
NORM_ROWS = 256
OUT_TM = 256
KV_CHUNK = 1280
ATT_ROWS = 1024


def _silu(x):
    return x / (1.0 + jnp.exp(-x))


def _cparams(sem):
    return pltpu.CompilerParams(dimension_semantics=sem, vmem_limit_bytes=VMEM_LIMIT_BYTES)


def _mod_kernel(s_ref, w_ref, b_ref, o_ref):
    s = _silu(s_ref[...])
    acc = jnp.dot(s.astype(BF16), w_ref[0].astype(BF16), preferred_element_type=F32)
    o_ref[0] = acc + b_ref[0]


def _modulation(s_in, w_mod, b_mod):
    depth, d, n = w_mod.shape
    tn = 768
    return pl.pallas_call(
        _mod_kernel,
        grid=(depth, n // tn),
        in_specs=[
            pl.BlockSpec((8, d), lambda l, j: (0, 0)),
            pl.BlockSpec((1, d, tn), lambda l, j: (l, 0, j)),
            pl.BlockSpec((1, 1, tn), lambda l, j: (l, 0, j)),
        ],
        out_specs=pl.BlockSpec((1, 8, tn), lambda l, j: (l, 0, j)),
        out_shape=jax.ShapeDtypeStruct((depth, 8, n), F32),
        compiler_params=_cparams(("parallel", "parallel")),
        name="modulation",
    )(s_in, w_mod, b_mod.reshape(depth, 1, n))


def _rope(x, cos, sin, quarter):
    lane = lax.broadcasted_iota(jnp.int32, x.shape, 1)
    first = (lane & quarter) == 0
    partner = jnp.where(first, pltpu.roll(x, LANES - quarter, 1), pltpu.roll(x, quarter, 1))
    return x * cos + partner * sin


def _proj_kernel(x_ref, mod_ref, g_ref, w_ref, cos_ref, sin_ref, qk_ref, o_ref, h_ref, *,
                 n_lat, tm, tn, even):
    i = pl.program_id(0)
    j = pl.program_id(1)
    d = x_ref.shape[1]

    @pl.when(j == 0)
    def _():
        def norm_rows(r, carry):
            r0 = pl.multiple_of(r * NORM_ROWS, NORM_ROWS)
            x = x_ref[pl.ds(r0, NORM_ROWS), :]
            ms = jnp.mean(x * x, axis=-1, keepdims=True)
            y = x * lax.rsqrt(ms + EPS) * g_ref[...]
            row = i * tm + r0 + lax.broadcasted_iota(jnp.int32, (NORM_ROWS, 1), 0)
            is_ctx = row >= n_lat
            shift = jnp.where(is_ctx, mod_ref[1:2, 0:d], mod_ref[0:1, 0:d])
            scale = jnp.where(is_ctx, mod_ref[1:2, d:2 * d], mod_ref[0:1, d:2 * d])
            h_ref[pl.ds(r0, NORM_ROWS), :] = (y * (1.0 + scale) + shift).astype(BF16)
            return carry

        lax.fori_loop(0, tm // NORM_ROWS, norm_rows, 0)

    acc = jnp.dot(h_ref[...], w_ref[...], preferred_element_type=F32)
    groups = tn // LANES

    if even:
        n_rope = 2 * (2 * A_HEADS * A_HEAD_DIM) // tn
        n_q = n_rope // 2
        bk_tile = (4 * A_WIDTH + B_HEADS * B_K_DIM) // tn

        @pl.when(j < n_rope)
        def _():
            cos = cos_ref[...]
            sin = sin_ref[...]
            qscale = jnp.where(j < n_q, A_HEAD_DIM ** -0.5, 1.0).astype(F32)
            for g in range(groups):
                sl = slice(g * LANES, (g + 1) * LANES)
                o_ref[:, sl] = (_rope(acc[:, sl], cos, sin, A_HEAD_DIM // 4) * qscale).astype(BF16)

        @pl.when(j == bk_tile)
        def _():
            o_ref[...] = (acc * (B_K_DIM ** -0.5)).astype(BF16)

        @pl.when(jnp.logical_and(j >= n_rope, j != bk_tile))
        def _():
            o_ref[...] = acc.astype(BF16)
    else:
        n_q = (C_HEADS * C_HEAD_DIM) // tn
        n_qk = n_q + (C_KV_HEADS * C_HEAD_DIM) // tn

        @pl.when(j < n_qk)
        def _():
            cos = cos_ref[...]
            sin = sin_ref[...]
            is_q = j < n_q
            gain = jnp.where(is_q, qk_ref[0:1, :], qk_ref[1:2, :])
            qscale = jnp.where(is_q, C_HEAD_DIM ** -0.5, 1.0).astype(F32)
            for g in range(groups):
                sl = slice(g * LANES, (g + 1) * LANES)
                xg = acc[:, sl]
                ms = jnp.mean(xg * xg, axis=-1, keepdims=True)
                yg = xg * lax.rsqrt(ms + EPS) * gain
                o_ref[:, sl] = (_rope(yg, cos, sin, C_HEAD_DIM // 4) * qscale).astype(BF16)

        @pl.when(j >= n_qk)
        def _():
            o_ref[...] = acc.astype(BF16)


def _proj(xs, mod_l, norm_g, w_in, cos, sin, qk_g, *, n_lat, even):
    t, d = xs.shape
    n = w_in.shape[1]
    tm, tn = PROJ_TM, PROJ_TN
    kern = functools.partial(_proj_kernel, n_lat=n_lat, tm=tm, tn=tn, even=even)
    return pl.pallas_call(
        kern,
        grid=(t // tm, n // tn),
        in_specs=[
            pl.BlockSpec((tm, d), lambda i, j: (i, 0)),
            pl.BlockSpec(mod_l.shape, lambda i, j: (0, 0)),
            pl.BlockSpec((1, d), lambda i, j: (0, 0)),
            pl.BlockSpec((d, tn), lambda i, j: (0, j)),
            pl.BlockSpec((tm, LANES), lambda i, j: (i, 0)),
            pl.BlockSpec((tm, LANES), lambda i, j: (i, 0)),
            pl.BlockSpec(qk_g.shape, lambda i, j: (0, 0)),
        ],
        out_specs=pl.BlockSpec((tm, tn), lambda i, j: (i, j)),
        out_shape=jax.ShapeDtypeStruct((t, n), BF16),
        scratch_shapes=[pltpu.VMEM((tm, d), BF16)],
        compiler_params=_cparams(("parallel", "arbitrary")),
        name="proj_even" if even else "proj_odd",
    )(xs, mod_l, norm_g.reshape(1, d), w_in, cos, sin, qk_g)


def _flash(qs, k_ref, v_ref, ck, n_chunks):
    m_rows = qs.shape[0]

    def body(c, carry):
        m, l, acc = carry
        off = pl.multiple_of(c * ck, ck)
        k = k_ref[pl.ds(off, ck), :]
        v = v_ref[pl.ds(off, ck), :]
        s = lax.dot_general(qs, k, (((1,), (1,)), ((), ())), preferred_element_type=F32)
        m_new = jnp.maximum(m, jnp.max(s, axis=1, keepdims=True))
        alpha = jnp.exp(m - m_new)
        p = jnp.exp(s - m_new)
        l = alpha * l + jnp.sum(p, axis=1, keepdims=True)
        acc = alpha * acc + jnp.dot(p.astype(BF16), v, preferred_element_type=F32)
        return m_new, l, acc

    init = (jnp.full((m_rows, 1), -jnp.inf, F32), jnp.zeros((m_rows, 1), F32),
            jnp.zeros((m_rows, v_ref.shape[1]), F32))
    _, l, acc = lax.fori_loop(0, n_chunks, body, init)
    return acc / l


def _gqa_kernel(q_ref, k_ref, v_ref, g_ref, o_ref, *, tq, ck, n_chunks):
    hd = C_HEAD_DIM
    qs = jnp.concatenate([q_ref[:, h * hd:(h + 1) * hd] for h in range(C_GROUP)], axis=0)
    o = _flash(qs, k_ref, v_ref, ck, n_chunks)
    for h in range(C_GROUP):
        gate = g_ref[:, h * hd:(h + 1) * hd].astype(F32)
        o_ref[:, h * hd:(h + 1) * hd] = (o[h * tq:(h + 1) * tq] * _silu(gate)).astype(BF16)


def _gqa_attention(p, y_prev, *, q_row0, n_q, k_row0, n_k):
    t = p.shape[0]
    tq = min(ATT_ROWS // C_GROUP, n_q)
    ck = min(KV_CHUNK, n_k)
    qb0, kb0 = q_row0 // tq, k_row0 // n_k
    w = C_GROUP * C_HEAD_DIM
    k_col0 = (C_HEADS * C_HEAD_DIM) // C_HEAD_DIM
    v_col0 = k_col0 + C_KV_HEADS
    g_col0 = (C_HEADS + 2 * C_KV_HEADS) * C_HEAD_DIM // w
    kern = functools.partial(_gqa_kernel, tq=tq, ck=ck, n_chunks=n_k // ck)
    in_specs = [
        pl.BlockSpec((tq, w), lambda h, i: (qb0 + i, h)),
        pl.BlockSpec((n_k, C_HEAD_DIM), lambda h, i: (kb0, k_col0 + h)),
        pl.BlockSpec((n_k, C_HEAD_DIM), lambda h, i: (kb0, v_col0 + h)),
        pl.BlockSpec((tq, w), lambda h, i: (qb0 + i, g_col0 + h)),
    ]
    args = [p, p, p, p]
    aliases = {}
    if y_prev is not None:
        in_specs.append(pl.BlockSpec(memory_space=pl.ANY))
        args.append(y_prev)
        aliases = {4: 0}
        kern = functools.partial(_drop_last_input, kern)
    return pl.pallas_call(
        kern,
        grid=(C_KV_HEADS, n_q // tq),
        in_specs=in_specs,
        out_specs=pl.BlockSpec((tq, w), lambda h, i: (qb0 + i, h)),
        out_shape=jax.ShapeDtypeStruct((t, D_MODEL), BF16),
        input_output_aliases=aliases,
        compiler_params=_cparams(("parallel", "arbitrary")),
        name="gqa_attention",
    )(*args)


def _drop_last_input(kern, *refs):
    return kern(*refs[:-2], refs[-1])


def _diff_kernel(lam_ref, sub_ref, q_ref, k_ref, v_ref, g_ref, o_ref, *, tq, ck, n_chunks,
                 lam_init):
    q = q_ref[...]
    lane = lax.broadcasted_iota(jnp.int32, q.shape, 1)
    zero = jnp.zeros_like(q)
    qs = jnp.concatenate([jnp.where(lane < A_HEAD_DIM, q, zero),
                          jnp.where(lane >= A_HEAD_DIM, q, zero)], axis=0)
    o = _flash(qs, k_ref, v_ref, ck, n_chunks)
    lp = lam_ref[...]
    lam = (jnp.exp(jnp.sum(lp[0:1] * lp[1:2], axis=-1, keepdims=True))
           - jnp.exp(jnp.sum(lp[2:3] * lp[3:4], axis=-1, keepdims=True)) + lam_init)
    od = o[:tq] - lam * o[tq:]
    ms = jnp.mean(od * od, axis=-1, keepdims=True)
    y = od * lax.rsqrt(ms + EPS) * sub_ref[...] * (1.0 - lam_init)
    o_ref[...] = (y * _silu(g_ref[...].astype(F32))).astype(BF16)


def _diff_attention(p, lam_p, subln, y_prev, *, lam_init, q_row0, n_q, k_row0, n_k):
    t = p.shape[0]
    tq = min(ATT_ROWS // 2, n_q)
    ck = min(KV_CHUNK, n_k)
    qb0, kb0 = q_row0 // tq, k_row0 // n_k
    w = A_V_DIM
    kern = functools.partial(_diff_kernel, tq=tq, ck=ck, n_chunks=n_k // ck, lam_init=lam_init)
    in_specs = [
        pl.BlockSpec(lam_p.shape, lambda h, i: (0, 0)),
        pl.BlockSpec((1, w), lambda h, i: (0, 0)),
        pl.BlockSpec((tq, w), lambda h, i: (qb0 + i, h)),
        pl.BlockSpec((n_k, w), lambda h, i: (kb0, A_HEADS + h)),
        pl.BlockSpec((n_k, w), lambda h, i: (kb0, 2 * A_HEADS + h)),
        pl.BlockSpec((tq, w), lambda h, i: (qb0 + i, 3 * A_HEADS + h)),
    ]
    args = [lam_p, subln.reshape(1, w), p, p, p, p]
    aliases = {}
    if y_prev is not None:
        in_specs.append(pl.BlockSpec(memory_space=pl.ANY))
        args.append(y_prev)
        aliases = {6: 0}
        kern = functools.partial(_drop_last_input, kern)
    return pl.pallas_call(
        kern,
        grid=(A_HEADS, n_q // tq),
        in_specs=in_specs,
        out_specs=pl.BlockSpec((tq, w), lambda h, i: (qb0 + i, h)),
        out_shape=jax.ShapeDtypeStruct((t, A_WIDTH), BF16),
        input_output_aliases=aliases,
        compiler_params=_cparams(("parallel", "arbitrary")),
        name="diff_attention",
    )(*args)


def _ret_kernel(a_ref, qf_ref, kf_ref, vf_ref, qb_ref, kb_ref, vb_ref, of_ref, ob_ref,
                s_ref, dec_ref, qd_ref, kd_ref, cd_ref):
    n = pl.program_id(0)
    c = RET_CHUNK

    @pl.when(n == 0)
    def _():
        i = lax.broadcasted_iota(jnp.int32, (c, c), 0).astype(F32)
        j = lax.broadcasted_iota(jnp.int32, (c, c), 1).astype(F32)
        for d in range(2):
            for h in range(B_HEADS):
                ch = d * B_HEADS + h
                lg = -jnp.exp(a_ref[ch:ch + 1, :])
                rel = (i - j) if d == 0 else (j - i)
                dec_ref[ch] = jnp.where(rel >= 0, jnp.exp(lg * jnp.maximum(rel, 0.0)), 0.0)
                qd_ref[ch] = jnp.exp(lg * ((i + 1.0) if d == 0 else (c - i)))
                kd_ref[ch] = jnp.exp(lg * ((c - 1.0 - i) if d == 0 else i))
                cd_ref[ch] = jnp.exp(lg * float(c)) + jnp.zeros((8, LANES), F32)
                s_ref[ch] = jnp.zeros(s_ref.shape[1:], F32)

    for d, (q_ref, k_ref, v_ref, o_ref) in enumerate(((qf_ref, kf_ref, vf_ref, of_ref),
                                                      (qb_ref, kb_ref, vb_ref, ob_ref))):
        for h in range(B_HEADS):
            ch = d * B_HEADS + h
            q = q_ref[:, h * B_K_DIM:(h + 1) * B_K_DIM]
            k = k_ref[:, h * B_K_DIM:(h + 1) * B_K_DIM]
            v = v_ref[:, h * B_V_DIM:(h + 1) * B_V_DIM]
            st = s_ref[ch]
            sc = lax.dot_general(q, k, (((1,), (1,)), ((), ())),
                                 preferred_element_type=F32) * dec_ref[ch]
            intra = jnp.dot(sc.astype(BF16), v, preferred_element_type=F32)
            qdec = (q.astype(F32) * qd_ref[ch]).astype(BF16)
            cross = jnp.dot(qdec, st.astype(BF16), preferred_element_type=F32)
            o_ref[:, h * B_V_DIM:(h + 1) * B_V_DIM] = intra + cross
            kdec = (k.astype(F32) * kd_ref[ch]).T.astype(BF16)
            upd = jnp.dot(kdec, v, preferred_element_type=F32)
            s_ref[ch] = st * cd_ref[ch][0:1, 0:1] + upd


def _retention(p, ret_decay, *, n_lat):
    t = p.shape[0]
    c = RET_CHUNK
    nc = t // c
    n_lat_c = n_lat // c
    qw, vw = B_HEADS * B_K_DIM, B_HEADS * B_V_DIM
    q_col = 4 * A_WIDTH // qw
    k_col = q_col + 1
    v_col = (4 * A_WIDTH + 2 * qw) // vw
    fwd = lambda n: (n + n_lat_c) % nc
    bwd = lambda n: nc - 1 - n
    return pl.pallas_call(
        _ret_kernel,
        grid=(nc,),
        in_specs=[
            pl.BlockSpec((2 * B_HEADS, 1), lambda n: (0, 0)),
            pl.BlockSpec((c, qw), lambda n: (fwd(n), q_col)),
            pl.BlockSpec((c, qw), lambda n: (fwd(n), k_col)),
            pl.BlockSpec((c, vw), lambda n: (fwd(n), v_col)),
            pl.BlockSpec((c, qw), lambda n: (bwd(n), q_col)),
            pl.BlockSpec((c, qw), lambda n: (bwd(n), k_col)),
            pl.BlockSpec((c, vw), lambda n: (bwd(n), v_col)),
        ],
        out_specs=[
            pl.BlockSpec((c, vw), lambda n: (fwd(n), 0)),
            pl.BlockSpec((c, vw), lambda n: (bwd(n), 0)),
        ],
        out_shape=[jax.ShapeDtypeStruct((t, vw), F32), jax.ShapeDtypeStruct((t, vw), F32)],
        scratch_shapes=[
            pltpu.VMEM((2 * B_HEADS, B_K_DIM, B_V_DIM), F32),
            pltpu.VMEM((2 * B_HEADS, c, c), F32),
            pltpu.VMEM((2 * B_HEADS, c, B_K_DIM), F32),
            pltpu.VMEM((2 * B_HEADS, c, B_K_DIM), F32),
            pltpu.VMEM((2 * B_HEADS, 8, LANES), F32),
        ],
        compiler_params=_cparams(("arbitrary",)),
        name="retention",
    )(ret_decay.reshape(2 * B_HEADS, 1), p, p, p, p, p, p)


def _out_tail(x_ref, y, mod_ref, fin_ref, o_ref, *, n_lat, tm, final):
    d = x_ref.shape[1]
    row = pl.program_id(0) * tm + lax.broadcasted_iota(jnp.int32, (tm, 1), 0)
    gate = jnp.where(row >= n_lat, mod_ref[1:2, 2 * d:3 * d], mod_ref[0:1, 2 * d:3 * d])
    xn = x_ref[...] + gate * y
    if final:
        ms = jnp.mean(xn * xn, axis=-1, keepdims=True)
        xn = xn * lax.rsqrt(ms + EPS) * fin_ref[...]
    o_ref[...] = xn


def _out_even_kernel(x_ref, ya_ref, of_ref, ob_ref, bg_ref, gn_ref, w_ref, mod_ref, fin_ref,
                     o_ref, *, n_lat, tm, final):
    ob = of_ref[...] + ob_ref[...]
    gate = _silu(bg_ref[...].astype(F32))
    parts = []
    for h in range(B_HEADS):
        sl = slice(h * B_V_DIM, (h + 1) * B_V_DIM)
        z = ob[:, sl]
        zc = z - jnp.mean(z, axis=-1, keepdims=True)
        yh = zc * lax.rsqrt(jnp.mean(zc * zc, axis=-1, keepdims=True) + EPS) * gn_ref[:, sl]
        parts.append((yh * gate[:, sl]).astype(BF16))
    yb = jnp.concatenate(parts, axis=1)
    ka = ya_ref.shape[1]
    y = (jnp.dot(ya_ref[...], w_ref[0:ka, :], preferred_element_type=F32)
         + jnp.dot(yb, w_ref[ka:, :], preferred_element_type=F32))
    _out_tail(x_ref, y, mod_ref, fin_ref, o_ref, n_lat=n_lat, tm=tm, final=final)


def _out_odd_kernel(x_ref, y_ref, w_ref, mod_ref, fin_ref, o_ref, *, n_lat, tm, final):
    y = jnp.dot(y_ref[...], w_ref[...], preferred_element_type=F32)
    _out_tail(x_ref, y, mod_ref, fin_ref, o_ref, n_lat=n_lat, tm=tm, final=final)


def _out_proj(xs, mixer_args, mixer_specs, w_out, mod_l, final_norm, *, n_lat, n_rows, even,
              final):
    t, d = xs.shape
    tm = OUT_TM
    body = _out_even_kernel if even else _out_odd_kernel
    kern = functools.partial(body, n_lat=n_lat, tm=tm, final=final)
    row_spec = pl.BlockSpec((tm, d), lambda i: (i, 0))
    const = lambda shape: pl.BlockSpec(shape, lambda i: (0, 0))
    return pl.pallas_call(
        kern,
        grid=(n_rows // tm,),
        in_specs=[row_spec] + mixer_specs + [const(w_out.shape), const(mod_l.shape), const((1, d))],
        out_specs=row_spec,
        out_shape=jax.ShapeDtypeStruct((n_rows, d), F32),
        compiler_params=_cparams(("parallel",)),
        name="out_proj",
    )(xs, *mixer_args, w_out, mod_l, final_norm.reshape(1, d))


def _rope_tables(n_lat, head_dim):
    pos = jnp.arange(n_lat, dtype=jnp.int32)
    rows = (pos // GRID_W).astype(F32)[:, None]
    cols = (pos % GRID_W).astype(F32)[:, None]
    axis_dim = head_dim // 2
    inv = ROPE_THETA ** (-jnp.arange(0, axis_dim, 2, dtype=F32) / axis_dim)
    ar, ac = rows * inv, cols * inv
    cos = jnp.concatenate([jnp.cos(ar), jnp.cos(ar), jnp.cos(ac), jnp.cos(ac)], axis=-1)
    sin = jnp.concatenate([-jnp.sin(ar), jnp.sin(ar), -jnp.sin(ac), jnp.sin(ac)], axis=-1)
    reps = LANES // head_dim
    cos, sin = jnp.tile(cos, (1, reps)), jnp.tile(sin, (1, reps))
    cos = jnp.concatenate([cos, jnp.ones((CTX_LEN, LANES), F32)], axis=0)
    sin = jnp.concatenate([sin, jnp.zeros((CTX_LEN, LANES), F32)], axis=0)
    return cos, sin


def kernel(x, c, ctx, c_ctx, norm_g, w_mod, b_mod, ev_w_in, ev_w_out, diff_lambda, diff_subln,
           ret_decay, ret_gn, od_w_in, od_w_out, qk_norm, final_norm):
    assert x.shape[0] == 1 and ctx.shape[1] == CTX_LEN and x.shape[2] == D_MODEL
    n_lat = x.shape[1]
    t = n_lat + CTX_LEN
    d = D_MODEL
    assert t % PROJ_TM == 0 and t % KV_CHUNK == 0 and n_lat % (ATT_ROWS // 2) == 0

    xs = jnp.concatenate([x[0], ctx[0]], axis=0)
    s_in = jnp.zeros((8, d), F32).at[0].set(c[0]).at[1].set(c_ctx)
    mod = _modulation(s_in, w_mod, b_mod)
    rope_a = _rope_tables(n_lat, A_HEAD_DIM)
    rope_c = _rope_tables(n_lat, C_HEAD_DIM)
    ev_w_in_b, ev_w_out_b = ev_w_in.astype(BF16), ev_w_out.astype(BF16)
    od_w_in_b, od_w_out_b = od_w_in.astype(BF16), od_w_out.astype(BF16)

    for i in range(DEPTH):
        final = i == DEPTH - 1
        n_rows = n_lat if final else t
        tm = OUT_TM
        if i % 2 == 0:
            e = i // 2
            lam_init = 0.8 - 0.6 * math.exp(-0.3 * i)
            p = _proj(xs, mod[i], norm_g[i], ev_w_in_b[e], *rope_a, qk_norm[0], n_lat=n_lat,
                      even=True)
            ya = _diff_attention(p, diff_lambda[e], diff_subln[e], None, lam_init=lam_init,
                                 q_row0=0, n_q=n_lat, k_row0=0, n_k=t)
            if not final:
                ya = _diff_attention(p, diff_lambda[e], diff_subln[e], ya, lam_init=lam_init,
                                     q_row0=n_lat, n_q=CTX_LEN, k_row0=n_lat, n_k=CTX_LEN)
            o_f, o_b = _retention(p, ret_decay[e], n_lat=n_lat)
            bw = B_WIDTH
            args = [ya, o_f, o_b, p, ret_gn[e].reshape(1, bw)]
            specs = [pl.BlockSpec((tm, A_WIDTH), lambda r: (r, 0)),
                     pl.BlockSpec((tm, bw), lambda r: (r, 0)),
                     pl.BlockSpec((tm, bw), lambda r: (r, 0)),
                     pl.BlockSpec((tm, bw), lambda r: (r, (EV_IN - bw) // bw)),
                     pl.BlockSpec((1, bw), lambda r: (0, 0))]
            xs = _out_proj(xs, args, specs, ev_w_out_b[e], mod[i], final_norm, n_lat=n_lat,
                           n_rows=n_rows, even=True, final=final)
        else:
            o = i // 2
            p = _proj(xs, mod[i], norm_g[i], od_w_in_b[o], *rope_c, qk_norm[o], n_lat=n_lat,
                      even=False)
            y = _gqa_attention(p, None, q_row0=0, n_q=n_lat, k_row0=0, n_k=t)
            if not final:
                y = _gqa_attention(p, y, q_row0=n_lat, n_q=CTX_LEN, k_row0=n_lat, n_k=CTX_LEN)
            specs = [pl.BlockSpec((tm, d), lambda r: (r, 0))]
            xs = _out_proj(xs, [y], specs, od_w_out_b[o], mod[i], final_norm, n_lat=n_lat,
                           n_rows=n_rows, even=False, final=final)
    return xs[None]
```

```python
import functools
import math

import jax
import jax.numpy as jnp
from jax import lax
from jax.experimental import pallas as pl
from jax.experimental.pallas import tpu as pltpu

F32 = jnp.float32
BF16 = jnp.bfloat16

D_MODEL = 2048
DEPTH = 4
GRID_W = 64
CTX_LEN = 256
RET_CHUNK = 128
ROPE_THETA = 10000.0
EPS = 1e-6

A_WIDTH = D_MODEL // 2
A_HEADS = 8
A_HEAD_DIM = A_WIDTH // A_HEADS // 2
A_V_DIM = 2 * A_HEAD_DIM
B_WIDTH = D_MODEL - A_WIDTH
B_HEADS = 4
B_V_DIM = B_WIDTH // B_HEADS
B_K_DIM = B_V_DIM // 2
EV_IN = 7168

C_HEADS = 16
C_KV_HEADS = 4
C_GROUP = C_HEADS // C_KV_HEADS
C_HEAD_DIM = D_MODEL // C_HEADS
OD_IN = 5120

LANES = 128
LOG2E = math.log2(math.e)
VMEM_LIMIT_BYTES = 56 * 1024 * 1024
PROJ_TM = 1280
PROJ_TN = 512
NORM_ROWS = 256
OUT_TM = 256
KV_CHUNK = 1280
ATT_ROWS = 1024


def _silu(x):
    return x / (1.0 + jnp.exp(-x))


def _cparams(sem):
    return pltpu.CompilerParams(dimension_semantics=sem, vmem_limit_bytes=VMEM_LIMIT_BYTES)


def _mod_kernel(s_ref, w_ref, b_ref, o_ref):
    s = _silu(s_ref[...])
    acc = jnp.dot(s.astype(BF16), w_ref[0].astype(BF16), preferred_element_type=F32)
    o_ref[0] = acc + b_ref[0]


def _modulation(s_in, w_mod, b_mod):
    depth, d, n = w_mod.shape
    tn = 768
    return pl.pallas_call(
        _mod_kernel,
        grid=(depth, n // tn),
        in_specs=[
            pl.BlockSpec((8, d), lambda l, j: (0, 0)),
            pl.BlockSpec((1, d, tn), lambda l, j: (l, 0, j)),
            pl.BlockSpec((1, 1, tn), lambda l, j: (l, 0, j)),
        ],
        out_specs=pl.BlockSpec((1, 8, tn), lambda l, j: (l, 0, j)),
        out_shape=jax.ShapeDtypeStruct((depth, 8, n), F32),
        compiler_params=_cparams(("parallel", "parallel")),
        name="modulation",
    )(s_in, w_mod, b_mod.reshape(depth, 1, n))


def _rope(x, cos, sin, quarter):
    lane = lax.broadcasted_iota(jnp.int32, x.shape, 1)
    first = (lane & quarter) == 0
    partner = jnp.where(first, pltpu.roll(x, LANES - quarter, 1), pltpu.roll(x, quarter, 1))
    return x * cos + partner * sin


def _proj_kernel(x_ref, mod_ref, g_ref, w_ref, cos_ref, sin_ref, qk_ref, o_ref, h_ref, *,
                 n_lat, tm, tn, even):
    i = pl.program_id(0)
    j = pl.program_id(1)
    d = x_ref.shape[1]

    @pl.when(j == 0)
    def _():
        def norm_rows(r, carry):
            r0 = pl.multiple_of(r * NORM_ROWS, NORM_ROWS)
            x = x_ref[pl.ds(r0, NORM_ROWS), :]
            ms = jnp.mean(x * x, axis=-1, keepdims=True)
            y = x * lax.rsqrt(ms + EPS) * g_ref[...]
            row = i * tm + r0 + lax.broadcasted_iota(jnp.int32, (NORM_ROWS, 1), 0)
            is_ctx = row >= n_lat
            shift = jnp.where(is_ctx, mod_ref[1:2, 0:d], mod_ref[0:1, 0:d])
            scale = jnp.where(is_ctx, mod_ref[1:2, d:2 * d], mod_ref[0:1, d:2 * d])
            h_ref[pl.ds(r0, NORM_ROWS), :] = (y * (1.0 + scale) + shift).astype(BF16)
            return carry

        lax.fori_loop(0, tm // NORM_ROWS, norm_rows, 0)

    acc = jnp.dot(h_ref[...], w_ref[...], preferred_element_type=F32)
    groups = tn // LANES

    if even:
        n_rope = 2 * (2 * A_HEADS * A_HEAD_DIM) // tn
        n_q = n_rope // 2
        bk_tile = (4 * A_WIDTH + B_HEADS * B_K_DIM) // tn

        @pl.when(j < n_rope)
        def _():
            cos = cos_ref[...]
            sin = sin_ref[...]
            qscale = jnp.where(j < n_q, A_HEAD_DIM ** -0.5 * LOG2E, 1.0).astype(F32)
            for g in range(groups):
                sl = slice(g * LANES, (g + 1) * LANES)
                o_ref[:, sl] = (_rope(acc[:, sl], cos, sin, A_HEAD_DIM // 4) * qscale).astype(BF16)

        @pl.when(j == bk_tile)
        def _():
            o_ref[...] = (acc * (B_K_DIM ** -0.5)).astype(BF16)

        @pl.when(jnp.logical_and(j >= n_rope, j != bk_tile))
        def _():
            o_ref[...] = acc.astype(BF16)
    else:
        n_q = (C_HEADS * C_HEAD_DIM) // tn
        n_qk = n_q + (C_KV_HEADS * C_HEAD_DIM) // tn

        @pl.when(j < n_qk)
        def _():
            cos = cos_ref[...]
            sin = sin_ref[...]
            is_q = j < n_q
            gain = jnp.where(is_q, qk_ref[0:1, :], qk_ref[1:2, :])
            qscale = jnp.where(is_q, C_HEAD_DIM ** -0.5 * LOG2E, 1.0).astype(F32)
            for g in range(groups):
                sl = slice(g * LANES, (g + 1) * LANES)
                xg = acc[:, sl]
                ms = jnp.mean(xg * xg, axis=-1, keepdims=True)
                yg = xg * lax.rsqrt(ms + EPS) * gain
                o_ref[:, sl] = (_rope(yg, cos, sin, C_HEAD_DIM // 4) * qscale).astype(BF16)

        @pl.when(j >= n_qk)
        def _():
            o_ref[...] = acc.astype(BF16)


def _proj(xs, mod_l, norm_g, w_in, cos, sin, qk_g, *, n_lat, even):
    t, d = xs.shape
    n = w_in.shape[1]
    tm, tn = PROJ_TM, PROJ_TN
    kern = functools.partial(_proj_kernel, n_lat=n_lat, tm=tm, tn=tn, even=even)
    return pl.pallas_call(
        kern,
        grid=(t // tm, n // tn),
        in_specs=[
            pl.BlockSpec((tm, d), lambda i, j: (i, 0)),
            pl.BlockSpec(mod_l.shape, lambda i, j: (0, 0)),
            pl.BlockSpec((1, d), lambda i, j: (0, 0)),
            pl.BlockSpec((d, tn), lambda i, j: (0, j)),
            pl.BlockSpec((tm, LANES), lambda i, j: (i, 0)),
            pl.BlockSpec((tm, LANES), lambda i, j: (i, 0)),
            pl.BlockSpec(qk_g.shape, lambda i, j: (0, 0)),
        ],
        out_specs=pl.BlockSpec((tm, tn), lambda i, j: (i, j)),
        out_shape=jax.ShapeDtypeStruct((t, n), BF16),
        scratch_shapes=[pltpu.VMEM((tm, d), BF16)],
        compiler_params=_cparams(("parallel", "arbitrary")),
        name="proj_even" if even else "proj_odd",
    )(xs, mod_l, norm_g.reshape(1, d), w_in, cos, sin, qk_g)


def _flash_t(qs, k_ref, vt_ref, s0_scr, s1_scr, m_scr, l_scr, acc_scr, ck, n_chunks):
    slots = (s0_scr, s1_scr)

    def scores(c, slot):
        off = pl.multiple_of(c * ck, ck)
        slots[slot][...] = lax.dot_general(k_ref[pl.ds(off, ck), :], qs,
                                           (((1,), (1,)), ((), ())),
                                           preferred_element_type=F32)

    def softmax_pv(c, slot):
        s = slots[slot][...]
        m_prev = m_scr[...]
        m_new = jnp.maximum(m_prev, jnp.max(s, axis=0, keepdims=True))
        alpha = jnp.exp2(m_prev - m_new)
        p = jnp.exp2(s - m_new)
        l_scr[...] = alpha * l_scr[...] + jnp.sum(p, axis=0, keepdims=True)
        m_scr[...] = m_new
        acc_scr[...] = alpha * acc_scr[...] + jnp.dot(vt_ref[c], p.astype(BF16),
                                                      preferred_element_type=F32)

    m_scr[...] = jnp.full(m_scr.shape, -jnp.inf, F32)
    l_scr[...] = jnp.zeros(l_scr.shape, F32)
    acc_scr[...] = jnp.zeros(acc_scr.shape, F32)
    scores(0, 0)

    def pair(i, carry):
        c0 = 2 * i
        scores(c0 + 1, 1)
        softmax_pv(c0, 0)
        scores(c0 + 2, 0)
        softmax_pv(c0 + 1, 1)
        return carry

    lax.fori_loop(0, (n_chunks - 1) // 2, pair, 0)
    if n_chunks % 2 == 1:
        softmax_pv(n_chunks - 1, 0)
    else:
        scores(n_chunks - 1, 1)
        softmax_pv(n_chunks - 2, 0)
        softmax_pv(n_chunks - 1, 1)
    return acc_scr[...], l_scr[...]


def _flash_scratch(m_rows, ck, dv):
    return [pltpu.VMEM((ck, m_rows), F32), pltpu.VMEM((ck, m_rows), F32),
            pltpu.VMEM((1, m_rows), F32), pltpu.VMEM((1, m_rows), F32),
            pltpu.VMEM((dv, m_rows), F32)]


def _v_transposed(p, col0, heads, dv, row0, n_k, ck):
    v = p[row0:row0 + n_k, col0:col0 + heads * dv]
    return v.reshape(n_k // ck, ck, heads, dv).transpose(2, 0, 3, 1)


def _gqa_kernel(q_ref, k_ref, vt_ref, g_ref, o_ref, *scratch, tq, ck, n_chunks):
    hd = C_HEAD_DIM
    qs = jnp.concatenate([q_ref[:, h * hd:(h + 1) * hd] for h in range(C_GROUP)], axis=0)
    acc, l = _flash_t(qs, k_ref, vt_ref, *scratch, ck, n_chunks)
    o_t = acc / l
    for h in range(C_GROUP):
        gate = g_ref[:, h * hd:(h + 1) * hd].astype(F32)
        o = o_t[:, h * tq:(h + 1) * tq].T
        o_ref[:, h * hd:(h + 1) * hd] = (o * _silu(gate)).astype(BF16)


def _skip_first_input(kern, *refs):
    return kern(*refs[1:])


def _gqa_attention(p, y_prev, *, q_row0, n_q, k_row0, n_k):
    t = p.shape[0]
    hd = C_HEAD_DIM
    tq = min(ATT_ROWS // C_GROUP, n_q)
    ck = min(KV_CHUNK, n_k)
    n_chunks = n_k // ck
    qb0, kb0 = q_row0 // tq, k_row0 // n_k
    w = C_GROUP * hd
    k_col0 = C_HEADS
    v_col0 = (C_HEADS + C_KV_HEADS) * hd
    g_col0 = (C_HEADS + 2 * C_KV_HEADS) * hd // w
    vt = _v_transposed(p, v_col0, C_KV_HEADS, hd, k_row0, n_k, ck)
    kern = functools.partial(_gqa_kernel, tq=tq, ck=ck, n_chunks=n_chunks)
    in_specs = [
        pl.BlockSpec((tq, w), lambda h, i: (qb0 + i, h)),
        pl.BlockSpec((n_k, hd), lambda h, i: (kb0, k_col0 + h)),
        pl.BlockSpec((None, n_chunks, hd, ck), lambda h, i: (h, 0, 0, 0)),
        pl.BlockSpec((tq, w), lambda h, i: (qb0 + i, g_col0 + h)),
    ]
    args = [p, p, vt, p]
    aliases = {}
    if y_prev is not None:
        in_specs.insert(0, pl.BlockSpec(memory_space=pl.ANY))
        args.insert(0, y_prev)
        aliases = {0: 0}
        kern = functools.partial(_skip_first_input, kern)
    return pl.pallas_call(
        kern,
        grid=(C_KV_HEADS, n_q // tq),
        in_specs=in_specs,
        out_specs=pl.BlockSpec((tq, w), lambda h, i: (qb0 + i, h)),
        out_shape=jax.ShapeDtypeStruct((t, D_MODEL), BF16),
        scratch_shapes=_flash_scratch(C_GROUP * tq, ck, hd),
        input_output_aliases=aliases,
        compiler_params=_cparams(("parallel", "arbitrary")),
        name="gqa_attention",
    )(*args)


def _diff_kernel(lam_ref, sub_ref, q_ref, k_ref, vt_ref, g_ref, o_ref, *scratch, tq, ck,
                 n_chunks, lam_init):
    q = q_ref[...]
    lane = lax.broadcasted_iota(jnp.int32, q.shape, 1)
    zero = jnp.zeros_like(q)
    qs = jnp.concatenate([jnp.where(lane < A_HEAD_DIM, q, zero),
                          jnp.where(lane >= A_HEAD_DIM, q, zero)], axis=0)
    acc, l = _flash_t(qs, k_ref, vt_ref, *scratch, ck, n_chunks)
    o_t = acc / l
    lp = lam_ref[...]
    lam = (jnp.exp(jnp.sum(lp[0:1] * lp[1:2], axis=-1, keepdims=True))
           - jnp.exp(jnp.sum(lp[2:3] * lp[3:4], axis=-1, keepdims=True)) + lam_init)
    od = (o_t[:, :tq] - lam * o_t[:, tq:]).T
    ms = jnp.mean(od * od, axis=-1, keepdims=True)
    y = od * lax.rsqrt(ms + EPS) * sub_ref[...] * (1.0 - lam_init)
    o_ref[...] = (y * _silu(g_ref[...].astype(F32))).astype(BF16)


def _diff_attention(p, lam_p, subln, y_prev, *, lam_init, q_row0, n_q, k_row0, n_k):
    t = p.shape[0]
    tq = min(ATT_ROWS // 2, n_q)
    ck = min(KV_CHUNK, n_k)
    n_chunks = n_k // ck
    qb0, kb0 = q_row0 // tq, k_row0 // n_k
    w = A_V_DIM
    vt = _v_transposed(p, 2 * A_HEADS * w, A_HEADS, w, k_row0, n_k, ck)
    kern = functools.partial(_diff_kernel, tq=tq, ck=ck, n_chunks=n_chunks, lam_init=lam_init)
    in_specs = [
        pl.BlockSpec(lam_p.shape, lambda h, i: (0, 0)),
        pl.BlockSpec((1, w), lambda h, i: (0, 0)),
        pl.BlockSpec((tq, w), lambda h, i: (qb0 + i, h)),
        pl.BlockSpec((n_k, w), lambda h, i: (kb0, A_HEADS + h)),
        pl.BlockSpec((None, n_chunks, w, ck), lambda h, i: (h, 0, 0, 0)),
        pl.BlockSpec((tq, w), lambda h, i: (qb0 + i, 3 * A_HEADS + h)),
    ]
    args = [lam_p, subln.reshape(1, w), p, p, vt, p]
    aliases = {}
    if y_prev is not None:
        in_specs.insert(0, pl.BlockSpec(memory_space=pl.ANY))
        args.insert(0, y_prev)
        aliases = {0: 0}
        kern = functools.partial(_skip_first_input, kern)
    return pl.pallas_call(
        kern,
        grid=(A_HEADS, n_q // tq),
        in_specs=in_specs,
        out_specs=pl.BlockSpec((tq, w), lambda h, i: (qb0 + i, h)),
        out_shape=jax.ShapeDtypeStruct((t, A_WIDTH), BF16),
        scratch_shapes=_flash_scratch(2 * tq, ck, w),
        input_output_aliases=aliases,
        compiler_params=_cparams(("parallel", "arbitrary")),
        name="diff_attention",
    )(*args)


def _ret_kernel(a_ref, qf_ref, kf_ref, vf_ref, qb_ref, kb_ref, vb_ref, of_ref, ob_ref,
                s_ref, dec_ref, qd_ref, kd_ref, cd_ref):
    n = pl.program_id(0)
    c = RET_CHUNK

    @pl.when(n == 0)
    def _():
        i = lax.broadcasted_iota(jnp.int32, (c, c), 0).astype(F32)
        j = lax.broadcasted_iota(jnp.int32, (c, c), 1).astype(F32)
        for d in range(2):
            for h in range(B_HEADS):
                ch = d * B_HEADS + h
                lg = -jnp.exp(a_ref[ch:ch + 1, :])
                rel = (i - j) if d == 0 else (j - i)
                dec_ref[ch] = jnp.where(rel >= 0, jnp.exp(lg * jnp.maximum(rel, 0.0)), 0.0)
                qd_ref[ch] = jnp.exp(lg * ((i + 1.0) if d == 0 else (c - i)))
                kd_ref[ch] = jnp.exp(lg * ((c - 1.0 - i) if d == 0 else i))
                cd_ref[ch] = jnp.exp(lg * float(c)) + jnp.zeros((8, LANES), F32)
                s_ref[ch] = jnp.zeros(s_ref.shape[1:], F32)

    for d, (q_ref, k_ref, v_ref, o_ref) in enumerate(((qf_ref, kf_ref, vf_ref, of_ref),
                                                      (qb_ref, kb_ref, vb_ref, ob_ref))):
        for h in range(B_HEADS):
            ch = d * B_HEADS + h
            q = q_ref[:, h * B_K_DIM:(h + 1) * B_K_DIM]
            k = k_ref[:, h * B_K_DIM:(h + 1) * B_K_DIM]
            v = v_ref[:, h * B_V_DIM:(h + 1) * B_V_DIM]
            st = s_ref[ch]
            sc = lax.dot_general(q, k, (((1,), (1,)), ((), ())),
                                 preferred_element_type=F32) * dec_ref[ch]
            intra = jnp.dot(sc.astype(BF16), v, preferred_element_type=F32)
            qdec = (q.astype(F32) * qd_ref[ch]).astype(BF16)
            cross = jnp.dot(qdec, st.astype(BF16), preferred_element_type=F32)
            o_ref[:, h * B_V_DIM:(h + 1) * B_V_DIM] = intra + cross
            kdec = (k.astype(F32) * kd_ref[ch]).T.astype(BF16)
            upd = jnp.dot(kdec, v, preferred_element_type=F32)
            s_ref[ch] = st * cd_ref[ch][0:1, 0:1] + upd


def _retention(p, ret_decay, *, n_lat):
    t = p.shape[0]
    c = RET_CHUNK
    nc = t // c
    n_lat_c = n_lat // c
    qw, vw = B_HEADS * B_K_DIM, B_HEADS * B_V_DIM
    q_col = 4 * A_WIDTH // qw
    k_col = q_col + 1
    v_col = (4 * A_WIDTH + 2 * qw) // vw
    fwd = lambda n: (n + n_lat_c) % nc
    bwd = lambda n: nc - 1 - n
    return pl.pallas_call(
        _ret_kernel,
        grid=(nc,),
        in_specs=[
            pl.BlockSpec((2 * B_HEADS, 1), lambda n: (0, 0)),
            pl.BlockSpec((c, qw), lambda n: (fwd(n), q_col)),
            pl.BlockSpec((c, qw), lambda n: (fwd(n), k_col)),
            pl.BlockSpec((c, vw), lambda n: (fwd(n), v_col)),
            pl.BlockSpec((c, qw), lambda n: (bwd(n), q_col)),
            pl.BlockSpec((c, qw), lambda n: (bwd(n), k_col)),
            pl.BlockSpec((c, vw), lambda n: (bwd(n), v_col)),
        ],
        out_specs=[
            pl.BlockSpec((c, vw), lambda n: (fwd(n), 0)),
            pl.BlockSpec((c, vw), lambda n: (bwd(n), 0)),
        ],
        out_shape=[jax.ShapeDtypeStruct((t, vw), F32), jax.ShapeDtypeStruct((t, vw), F32)],
        scratch_shapes=[
            pltpu.VMEM((2 * B_HEADS, B_K_DIM, B_V_DIM), F32),
            pltpu.VMEM((2 * B_HEADS, c, c), F32),
            pltpu.VMEM((2 * B_HEADS, c, B_K_DIM), F32),
            pltpu.VMEM((2 * B_HEADS, c, B_K_DIM), F32),
            pltpu.VMEM((2 * B_HEADS, 8, LANES), F32),
        ],
        compiler_params=_cparams(("arbitrary",)),
        name="retention",
    )(ret_decay.reshape(2 * B_HEADS, 1), p, p, p, p, p, p)


def _out_tail(x_ref, y, mod_ref, fin_ref, o_ref, *, n_lat, tm, final):
    d = x_ref.shape[1]
    row = pl.program_id(0) * tm + lax.broadcasted_iota(jnp.int32, (tm, 1), 0)
    gate = jnp.where(row >= n_lat, mod_ref[1:2, 2 * d:3 * d], mod_ref[0:1, 2 * d:3 * d])
    xn = x_ref[...] + gate * y
    if final:
        ms = jnp.mean(xn * xn, axis=-1, keepdims=True)
        xn = xn * lax.rsqrt(ms + EPS) * fin_ref[...]
    o_ref[...] = xn


def _out_even_kernel(x_ref, ya_ref, of_ref, ob_ref, bg_ref, gn_ref, w_ref, mod_ref, fin_ref,
                     o_ref, *, n_lat, tm, final):
    ob = of_ref[...] + ob_ref[...]
    gate = _silu(bg_ref[...].astype(F32))
    parts = []
    for h in range(B_HEADS):
        sl = slice(h * B_V_DIM, (h + 1) * B_V_DIM)
        z = ob[:, sl]
        zc = z - jnp.mean(z, axis=-1, keepdims=True)
        yh = zc * lax.rsqrt(jnp.mean(zc * zc, axis=-1, keepdims=True) + EPS) * gn_ref[:, sl]
        parts.append((yh * gate[:, sl]).astype(BF16))
    yb = jnp.concatenate(parts, axis=1)
    ka = ya_ref.shape[1]
    y = (jnp.dot(ya_ref[...], w_ref[0:ka, :], preferred_element_type=F32)
         + jnp.dot(yb, w_ref[ka:, :], preferred_element_type=F32))
    _out_tail(x_ref, y, mod_ref, fin_ref, o_ref, n_lat=n_lat, tm=tm, final=final)


def _out_odd_kernel(x_ref, y_ref, w_ref, mod_ref, fin_ref, o_ref, *, n_lat, tm, final):
    y = jnp.dot(y_ref[...], w_ref[...], preferred_element_type=F32)
    _out_tail(x_ref, y, mod_ref, fin_ref, o_ref, n_lat=n_lat, tm=tm, final=final)


def _out_proj(xs, mixer_args, mixer_specs, w_out, mod_l, final_norm, *, n_lat, n_rows, even,
              final):
    t, d = xs.shape
    tm = OUT_TM
    body = _out_even_kernel if even else _out_odd_kernel
    kern = functools.partial(body, n_lat=n_lat, tm=tm, final=final)
    row_spec = pl.BlockSpec((tm, d), lambda i: (i, 0))
    const = lambda shape: pl.BlockSpec(shape, lambda i: (0, 0))
    return pl.pallas_call(
        kern,
        grid=(n_rows // tm,),
        in_specs=[row_spec] + mixer_specs + [const(w_out.shape), const(mod_l.shape), const((1, d))],
        out_specs=row_spec,
        out_shape=jax.ShapeDtypeStruct((n_rows, d), F32),
        compiler_params=_cparams(("parallel",)),
        name="out_proj",
    )(xs, *mixer_args, w_out, mod_l, final_norm.reshape(1, d))


def _rope_tables(n_lat, head_dim):
    pos = jnp.arange(n_lat, dtype=jnp.int32)
    rows = (pos // GRID_W).astype(F32)[:, None]
    cols = (pos % GRID_W).astype(F32)[:, None]
    axis_dim = head_dim // 2
    inv = ROPE_THETA ** (-jnp.arange(0, axis_dim, 2, dtype=F32) / axis_dim)
    ar, ac = rows * inv, cols * inv
    cos = jnp.concatenate([jnp.cos(ar), jnp.cos(ar), jnp.cos(ac), jnp.cos(ac)], axis=-1)
    sin = jnp.concatenate([-jnp.sin(ar), jnp.sin(ar), -jnp.sin(ac), jnp.sin(ac)], axis=-1)
    reps = LANES // head_dim
    cos, sin = jnp.tile(cos, (1, reps)), jnp.tile(sin, (1, reps))
    cos = jnp.concatenate([cos, jnp.ones((CTX_LEN, LANES), F32)], axis=0)
    sin = jnp.concatenate([sin, jnp.zeros((CTX_LEN, LANES), F32)], axis=0)
    return cos, sin


def kernel(x, c, ctx, c_ctx, norm_g, w_mod, b_mod, ev_w_in, ev_w_out, diff_lambda, diff_subln,
           ret_decay, ret_gn, od_w_in, od_w_out, qk_norm, final_norm):
    assert x.shape[0] == 1 and ctx.shape[1] == CTX_LEN and x.shape[2] == D_MODEL
    n_lat = x.shape[1]
    t = n_lat + CTX_LEN
    d = D_MODEL
    assert t % PROJ_TM == 0 and t % KV_CHUNK == 0 and n_lat % (ATT_ROWS // 2) == 0

    xs = jnp.concatenate([x[0], ctx[0]], axis=0)
    s_in = jnp.zeros((8, d), F32).at[0].set(c[0]).at[1].set(c_ctx)
    mod = _modulation(s_in, w_mod, b_mod)
    rope_a = _rope_tables(n_lat, A_HEAD_DIM)
    rope_c = _rope_tables(n_lat, C_HEAD_DIM)
    ev_w_in_b, ev_w_out_b = ev_w_in.astype(BF16), ev_w_out.astype(BF16)
    od_w_in_b, od_w_out_b = od_w_in.astype(BF16), od_w_out.astype(BF16)

    for i in range(DEPTH):
        final = i == DEPTH - 1
        n_rows = n_lat if final else t
        tm = OUT_TM
        if i % 2 == 0:
            e = i // 2
            lam_init = 0.8 - 0.6 * math.exp(-0.3 * i)
            p = _proj(xs, mod[i], norm_g[i], ev_w_in_b[e], *rope_a, qk_norm[0], n_lat=n_lat,
                      even=True)
            ya = _diff_attention(p, diff_lambda[e], diff_subln[e], None, lam_init=lam_init,
                                 q_row0=0, n_q=n_lat, k_row0=0, n_k=t)
            if not final:
                ya = _diff_attention(p, diff_lambda[e], diff_subln[e], ya, lam_init=lam_init,
                                     q_row0=n_lat, n_q=CTX_LEN, k_row0=n_lat, n_k=CTX_LEN)
            o_f, o_b = _retention(p, ret_decay[e], n_lat=n_lat)
            bw = B_WIDTH
            args = [ya, o_f, o_b, p, ret_gn[e].reshape(1, bw)]
            specs = [pl.BlockSpec((tm, A_WIDTH), lambda r: (r, 0)),
                     pl.BlockSpec((tm, bw), lambda r: (r, 0)),
                     pl.BlockSpec((tm, bw), lambda r: (r, 0)),
                     pl.BlockSpec((tm, bw), lambda r: (r, (EV_IN - bw) // bw)),
                     pl.BlockSpec((1, bw), lambda r: (0, 0))]
            xs = _out_proj(xs, args, specs, ev_w_out_b[e], mod[i], final_norm, n_lat=n_lat,
                           n_rows=n_rows, even=True, final=final)
        else:
            o = i // 2
            p = _proj(xs, mod[i], norm_g[i], od_w_in_b[o], *rope_c, qk_norm[o], n_lat=n_lat,
                      even=False)
            y = _gqa_attention(p, None, q_row0=0, n_q=n_lat, k_row0=0, n_k=t)
            if not final:
                y = _gqa_attention(p, y, q_row0=n_lat, n_q=CTX_LEN, k_row0=n_lat, n_k=CTX_LEN)
            specs = [pl.BlockSpec((tm, d), lambda r: (r, 0))]
            xs = _out_proj(xs, [y], specs, od_w_out_b[o], mod[i], final_norm, n_lat=n_lat,
                           n_rows=n_rows, even=False, final=final)
    return xs[None]
```

```python
import functools
import math

import jax
import jax.numpy as jnp
from jax import lax
from jax.experimental import pallas as pl
from jax.experimental.pallas import tpu as pltpu

F32 = jnp.float32
BF16 = jnp.bfloat16

D_MODEL = 2048
DEPTH = 4
GRID_W = 64
CTX_LEN = 256
RET_CHUNK = 128
ROPE_THETA = 10000.0
EPS = 1e-6

A_WIDTH = D_MODEL // 2
A_HEADS = 8
A_HEAD_DIM = A_WIDTH // A_HEADS // 2
A_V_DIM = 2 * A_HEAD_DIM
B_WIDTH = D_MODEL - A_WIDTH
B_HEADS = 4
B_V_DIM = B_WIDTH // B_HEADS
B_K_DIM = B_V_DIM // 2
EV_IN = 7168

C_HEADS = 16
C_KV_HEADS = 4
C_GROUP = C_HEADS // C_KV_HEADS
C_HEAD_DIM = D_MODEL // C_HEADS
OD_IN = 5120

LANES = 128
SUBLANES = 8
SOFTMAX_ROWS = 16
MAX_CHAINS = 4
LOG2E = math.log2(math.e)
VMEM_LIMIT_BYTES = 56 * 1024 * 1024
PROJ_TM = 1280
PROJ_TN = 512
NORM_ROWS = 256
OUT_TM = 256
KV_CHUNK = 1280
ATT_ROWS = 1024


def _silu(x):
    return x / (1.0 + jnp.exp(-x))


def _cparams(sem):
    return pltpu.CompilerParams(dimension_semantics=sem, vmem_limit_bytes=VMEM_LIMIT_BYTES)


def _mod_kernel(s_ref, w_ref, b_ref, o_ref):
    s = _silu(s_ref[...])
    acc = jnp.dot(s.astype(BF16), w_ref[0].astype(BF16), preferred_element_type=F32)
    o_ref[0] = acc + b_ref[0]


def _modulation(s_in, w_mod, b_mod):
    depth, d, n = w_mod.shape
    tn = 768
    return pl.pallas_call(
        _mod_kernel,
        grid=(depth, n // tn),
        in_specs=[
            pl.BlockSpec((8, d), lambda l, j: (0, 0)),
            pl.BlockSpec((1, d, tn), lambda l, j: (l, 0, j)),
            pl.BlockSpec((1, 1, tn), lambda l, j: (l, 0, j)),
        ],
        out_specs=pl.BlockSpec((1, 8, tn), lambda l, j: (l, 0, j)),
        out_shape=jax.ShapeDtypeStruct((depth, 8, n), F32),
        compiler_params=_cparams(("parallel", "parallel")),
        name="modulation",
    )(s_in, w_mod, b_mod.reshape(depth, 1, n))


def _rope(x, cos, sin, quarter):
    lane = lax.broadcasted_iota(jnp.int32, x.shape, 1)
    first = (lane & quarter) == 0
    partner = jnp.where(first, pltpu.roll(x, LANES - quarter, 1), pltpu.roll(x, quarter, 1))
    return x * cos + partner * sin


def _proj_kernel(x_ref, mod_ref, g_ref, w_ref, cos_ref, sin_ref, qk_ref, o_ref, h_ref, *,
                 n_lat, tm, tn, even):
    i = pl.program_id(0)
    j = pl.program_id(1)
    d = x_ref.shape[1]

    @pl.when(j == 0)
    def _():
        def norm_rows(r, carry):
            r0 = pl.multiple_of(r * NORM_ROWS, NORM_ROWS)
            x = x_ref[pl.ds(r0, NORM_ROWS), :]
            ms = jnp.mean(x * x, axis=-1, keepdims=True)
            y = x * lax.rsqrt(ms + EPS) * g_ref[...]
            row = i * tm + r0 + lax.broadcasted_iota(jnp.int32, (NORM_ROWS, 1), 0)
            is_ctx = row >= n_lat
            shift = jnp.where(is_ctx, mod_ref[1:2, 0:d], mod_ref[0:1, 0:d])
            scale = jnp.where(is_ctx, mod_ref[1:2, d:2 * d], mod_ref[0:1, d:2 * d])
            h_ref[pl.ds(r0, NORM_ROWS), :] = (y * (1.0 + scale) + shift).astype(BF16)
            return carry

        lax.fori_loop(0, tm // NORM_ROWS, norm_rows, 0)

    acc = jnp.dot(h_ref[...], w_ref[...], preferred_element_type=F32)
    groups = tn // LANES

    if even:
        n_rope = 2 * (2 * A_HEADS * A_HEAD_DIM) // tn
        n_q = n_rope // 2
        bk_tile = (4 * A_WIDTH + B_HEADS * B_K_DIM) // tn

        @pl.when(j < n_rope)
        def _():
            cos = cos_ref[...]
            sin = sin_ref[...]
            qscale = jnp.where(j < n_q, A_HEAD_DIM ** -0.5 * LOG2E, 1.0).astype(F32)
            for g in range(groups):
                sl = slice(g * LANES, (g + 1) * LANES)
                o_ref[:, sl] = (_rope(acc[:, sl], cos, sin, A_HEAD_DIM // 4) * qscale).astype(BF16)

        @pl.when(j == bk_tile)
        def _():
            o_ref[...] = (acc * (B_K_DIM ** -0.5)).astype(BF16)

        @pl.when(jnp.logical_and(j >= n_rope, j != bk_tile))
        def _():
            o_ref[...] = acc.astype(BF16)
    else:
        n_q = (C_HEADS * C_HEAD_DIM) // tn
        n_qk = n_q + (C_KV_HEADS * C_HEAD_DIM) // tn

        @pl.when(j < n_qk)
        def _():
            cos = cos_ref[...]
            sin = sin_ref[...]
            is_q = j < n_q
            gain = jnp.where(is_q, qk_ref[0:1, :], qk_ref[1:2, :])
            qscale = jnp.where(is_q, C_HEAD_DIM ** -0.5 * LOG2E, 1.0).astype(F32)
            for g in range(groups):
                sl = slice(g * LANES, (g + 1) * LANES)
                xg = acc[:, sl]
                ms = jnp.mean(xg * xg, axis=-1, keepdims=True)
                yg = xg * lax.rsqrt(ms + EPS) * gain
                o_ref[:, sl] = (_rope(yg, cos, sin, C_HEAD_DIM // 4) * qscale).astype(BF16)

        @pl.when(j >= n_qk)
        def _():
            o_ref[...] = acc.astype(BF16)


def _proj(xs, mod_l, norm_g, w_in, cos, sin, qk_g, *, n_lat, even):
    t, d = xs.shape
    n = w_in.shape[1]
    tm, tn = PROJ_TM, PROJ_TN
    kern = functools.partial(_proj_kernel, n_lat=n_lat, tm=tm, tn=tn, even=even)
    return pl.pallas_call(
        kern,
        grid=(t // tm, n // tn),
        in_specs=[
            pl.BlockSpec((tm, d), lambda i, j: (i, 0)),
            pl.BlockSpec(mod_l.shape, lambda i, j: (0, 0)),
            pl.BlockSpec((1, d), lambda i, j: (0, 0)),
            pl.BlockSpec((d, tn), lambda i, j: (0, j)),
            pl.BlockSpec((tm, LANES), lambda i, j: (i, 0)),
            pl.BlockSpec((tm, LANES), lambda i, j: (i, 0)),
            pl.BlockSpec(qk_g.shape, lambda i, j: (0, 0)),
        ],
        out_specs=pl.BlockSpec((tm, tn), lambda i, j: (i, j)),
        out_shape=jax.ShapeDtypeStruct((t, n), BF16),
        scratch_shapes=[pltpu.VMEM((tm, d), BF16)],
        compiler_params=_cparams(("parallel", "arbitrary")),
        name="proj_even" if even else "proj_odd",
    )(xs, mod_l, norm_g.reshape(1, d), w_in, cos, sin, qk_g)


def _flash_t(qs, k_ref, vt_ref, s_scr, mx_scr, p_scr, a_scr, m_scr, l_scr, acc_scr, ck,
             n_chunks):
    sub = SUBLANES
    rb = SOFTMAX_ROWS

    def scores(c, slot):
        off = pl.multiple_of(c * ck, ck)
        s = lax.dot_general(k_ref[pl.ds(off, ck), :], qs, (((1,), (1,)), ((), ())),
                            preferred_element_type=F32)
        s_scr[slot][...] = s
        parts = [s[t * sub:(t + 1) * sub] for t in range(MAX_CHAINS)]
        for r in range(MAX_CHAINS, ck // sub):
            parts[r % MAX_CHAINS] = jnp.maximum(parts[r % MAX_CHAINS], s[r * sub:(r + 1) * sub])
        mx_scr[slot][...] = functools.reduce(jnp.maximum, parts)

    def softmax(slot):
        m_prev = m_scr[...]
        m_new = jnp.maximum(m_prev, jnp.max(mx_scr[slot][...], axis=0, keepdims=True))
        alpha = jnp.exp2(m_prev - m_new)
        m_scr[...] = m_new
        a_scr[slot][...] = alpha
        m_blk = jnp.broadcast_to(m_new, (rb, m_new.shape[1]))
        l_part = jnp.zeros(l_scr.shape, F32)
        for r in range(ck // rb):
            rows = slice(r * rb, (r + 1) * rb)
            pb = jnp.exp2(s_scr[slot][rows, :] - m_blk)
            for t in range(rb // sub):
                l_part = l_part + pb[t * sub:(t + 1) * sub]
            p_scr[slot][rows, :] = pb.astype(BF16)
        l_scr[...] = alpha * l_scr[...] + l_part

    def pv(c, slot):
        acc_scr[...] = a_scr[slot][...] * acc_scr[...] + jnp.dot(
            vt_ref[c], p_scr[slot][...], preferred_element_type=F32)

    def step(c, parity, with_scores, with_softmax):
        if with_scores:
            scores(c + 2, parity)
        if with_softmax:
            softmax(1 - parity)
        pv(c, parity)

    m_scr[...] = jnp.full(m_scr.shape, -jnp.inf, F32)
    l_scr[...] = jnp.zeros(l_scr.shape, F32)
    acc_scr[...] = jnp.zeros(acc_scr.shape, F32)
    scores(0, 0)
    if n_chunks > 1:
        scores(1, 1)
    softmax(0)

    n_pairs = max(n_chunks - 2, 0) // 2

    def pair(i, carry):
        step(2 * i, 0, True, True)
        step(2 * i + 1, 1, True, True)
        return carry

    lax.fori_loop(0, n_pairs, pair, 0)
    for c in range(2 * n_pairs, n_chunks):
        step(c, c % 2, c + 2 < n_chunks, c + 1 < n_chunks)
    return acc_scr[...], jnp.sum(l_scr[...], axis=0, keepdims=True)


def _flash_scratch(m_rows, ck, dv):
    two = lambda shape, dtype: [pltpu.VMEM(shape, dtype), pltpu.VMEM(shape, dtype)]
    return [two((ck, m_rows), F32), two((SUBLANES, m_rows), F32), two((ck, m_rows), BF16),
            two((1, m_rows), F32), pltpu.VMEM((1, m_rows), F32),
            pltpu.VMEM((SUBLANES, m_rows), F32), pltpu.VMEM((dv, m_rows), F32)]


def _v_transposed(p, col0, heads, dv, row0, n_k, ck):
    v = p[row0:row0 + n_k, col0:col0 + heads * dv]
    return v.reshape(n_k // ck, ck, heads, dv).transpose(2, 0, 3, 1)


def _gqa_kernel(q_ref, k_ref, vt_ref, g_ref, o_ref, *scratch, tq, ck, n_chunks):
    hd = C_HEAD_DIM
    qs = jnp.concatenate([q_ref[:, h * hd:(h + 1) * hd] for h in range(C_GROUP)], axis=0)
    acc, l = _flash_t(qs, k_ref, vt_ref, *scratch, ck, n_chunks)
    o_t = acc / l
    for h in range(C_GROUP):
        gate = g_ref[:, h * hd:(h + 1) * hd].astype(F32)
        o = o_t[:, h * tq:(h + 1) * tq].T
        o_ref[:, h * hd:(h + 1) * hd] = (o * _silu(gate)).astype(BF16)


def _skip_first_input(kern, *refs):
    return kern(*refs[1:])


def _gqa_attention(p, y_prev, *, q_row0, n_q, k_row0, n_k):
    t = p.shape[0]
    hd = C_HEAD_DIM
    tq = min(ATT_ROWS // C_GROUP, n_q)
    ck = min(KV_CHUNK, n_k)
    n_chunks = n_k // ck
    qb0, kb0 = q_row0 // tq, k_row0 // n_k
    w = C_GROUP * hd
    k_col0 = C_HEADS
    v_col0 = (C_HEADS + C_KV_HEADS) * hd
    g_col0 = (C_HEADS + 2 * C_KV_HEADS) * hd // w
    vt = _v_transposed(p, v_col0, C_KV_HEADS, hd, k_row0, n_k, ck)
    kern = functools.partial(_gqa_kernel, tq=tq, ck=ck, n_chunks=n_chunks)
    in_specs = [
        pl.BlockSpec((tq, w), lambda h, i: (qb0 + i, h)),
        pl.BlockSpec((n_k, hd), lambda h, i: (kb0, k_col0 + h)),
        pl.BlockSpec((None, n_chunks, hd, ck), lambda h, i: (h, 0, 0, 0)),
        pl.BlockSpec((tq, w), lambda h, i: (qb0 + i, g_col0 + h)),
    ]
    args = [p, p, vt, p]
    aliases = {}
    if y_prev is not None:
        in_specs.insert(0, pl.BlockSpec(memory_space=pl.ANY))
        args.insert(0, y_prev)
        aliases = {0: 0}
        kern = functools.partial(_skip_first_input, kern)
    return pl.pallas_call(
        kern,
        grid=(C_KV_HEADS, n_q // tq),
        in_specs=in_specs,
        out_specs=pl.BlockSpec((tq, w), lambda h, i: (qb0 + i, h)),
        out_shape=jax.ShapeDtypeStruct((t, D_MODEL), BF16),
        scratch_shapes=_flash_scratch(C_GROUP * tq, ck, hd),
        input_output_aliases=aliases,
        compiler_params=_cparams(("parallel", "arbitrary")),
        name="gqa_attention",
    )(*args)


def _diff_kernel(lam_ref, sub_ref, q_ref, k_ref, vt_ref, g_ref, o_ref, *scratch, tq, ck,
                 n_chunks, lam_init):
    q = q_ref[...]
    lane = lax.broadcasted_iota(jnp.int32, q.shape, 1)
    zero = jnp.zeros_like(q)
    qs = jnp.concatenate([jnp.where(lane < A_HEAD_DIM, q, zero),
                          jnp.where(lane >= A_HEAD_DIM, q, zero)], axis=0)
    acc, l = _flash_t(qs, k_ref, vt_ref, *scratch, ck, n_chunks)
    o_t = acc / l
    lp = lam_ref[...]
    lam = (jnp.exp(jnp.sum(lp[0:1] * lp[1:2], axis=-1, keepdims=True))
           - jnp.exp(jnp.sum(lp[2:3] * lp[3:4], axis=-1, keepdims=True)) + lam_init)
    od = (o_t[:, :tq] - lam * o_t[:, tq:]).T
    ms = jnp.mean(od * od, axis=-1, keepdims=True)
    y = od * lax.rsqrt(ms + EPS) * sub_ref[...] * (1.0 - lam_init)
    o_ref[...] = (y * _silu(g_ref[...].astype(F32))).astype(BF16)


def _diff_attention(p, lam_p, subln, y_prev, *, lam_init, q_row0, n_q, k_row0, n_k):
    t = p.shape[0]
    tq = min(ATT_ROWS // 2, n_q)
    ck = min(KV_CHUNK, n_k)
    n_chunks = n_k // ck
    qb0, kb0 = q_row0 // tq, k_row0 // n_k
    w = A_V_DIM
    vt = _v_transposed(p, 2 * A_HEADS * w, A_HEADS, w, k_row0, n_k, ck)
    kern = functools.partial(_diff_kernel, tq=tq, ck=ck, n_chunks=n_chunks, lam_init=lam_init)
    in_specs = [
        pl.BlockSpec(lam_p.shape, lambda h, i: (0, 0)),
        pl.BlockSpec((1, w), lambda h, i: (0, 0)),
        pl.BlockSpec((tq, w), lambda h, i: (qb0 + i, h)),
        pl.BlockSpec((n_k, w), lambda h, i: (kb0, A_HEADS + h)),
        pl.BlockSpec((None, n_chunks, w, ck), lambda h, i: (h, 0, 0, 0)),
        pl.BlockSpec((tq, w), lambda h, i: (qb0 + i, 3 * A_HEADS + h)),
    ]
    args = [lam_p, subln.reshape(1, w), p, p, vt, p]
    aliases = {}
    if y_prev is not None:
        in_specs.insert(0, pl.BlockSpec(memory_space=pl.ANY))
        args.insert(0, y_prev)
        aliases = {0: 0}
        kern = functools.partial(_skip_first_input, kern)
    return pl.pallas_call(
        kern,
        grid=(A_HEADS, n_q // tq),
        in_specs=in_specs,
        out_specs=pl.BlockSpec((tq, w), lambda h, i: (qb0 + i, h)),
        out_shape=jax.ShapeDtypeStruct((t, A_WIDTH), BF16),
        scratch_shapes=_flash_scratch(2 * tq, ck, w),
        input_output_aliases=aliases,
        compiler_params=_cparams(("parallel", "arbitrary")),
        name="diff_attention",
    )(*args)


def _ret_kernel(a_ref, qf_ref, kf_ref, vf_ref, qb_ref, kb_ref, vb_ref, of_ref, ob_ref,
                s_ref, dec_ref, qd_ref, kd_ref, cd_ref):
    n = pl.program_id(0)
    c = RET_CHUNK

    @pl.when(n == 0)
    def _():
        i = lax.broadcasted_iota(jnp.int32, (c, c), 0).astype(F32)
        j = lax.broadcasted_iota(jnp.int32, (c, c), 1).astype(F32)
        for d in range(2):
            for h in range(B_HEADS):
                ch = d * B_HEADS + h
                lg = -jnp.exp(a_ref[ch:ch + 1, :])
                rel = (i - j) if d == 0 else (j - i)
                dec_ref[ch] = jnp.where(rel >= 0, jnp.exp(lg * jnp.maximum(rel, 0.0)), 0.0)
                qd_ref[ch] = jnp.exp(lg * ((i + 1.0) if d == 0 else (c - i)))
                kd_ref[ch] = jnp.exp(lg * ((c - 1.0 - i) if d == 0 else i))
                cd_ref[ch] = jnp.exp(lg * float(c)) + jnp.zeros((8, LANES), F32)
                s_ref[ch] = jnp.zeros(s_ref.shape[1:], F32)

    for d, (q_ref, k_ref, v_ref, o_ref) in enumerate(((qf_ref, kf_ref, vf_ref, of_ref),
                                                      (qb_ref, kb_ref, vb_ref, ob_ref))):
        for h in range(B_HEADS):
            ch = d * B_HEADS + h
            q = q_ref[:, h * B_K_DIM:(h + 1) * B_K_DIM]
            k = k_ref[:, h * B_K_DIM:(h + 1) * B_K_DIM]
            v = v_ref[:, h * B_V_DIM:(h + 1) * B_V_DIM]
            st = s_ref[ch]
            sc = lax.dot_general(q, k, (((1,), (1,)), ((), ())),
                                 preferred_element_type=F32) * dec_ref[ch]
            intra = jnp.dot(sc.astype(BF16), v, preferred_element_type=F32)
            qdec = (q.astype(F32) * qd_ref[ch]).astype(BF16)
            cross = jnp.dot(qdec, st.astype(BF16), preferred_element_type=F32)
            o_ref[:, h * B_V_DIM:(h + 1) * B_V_DIM] = intra + cross
            kdec = (k.astype(F32) * kd_ref[ch]).T.astype(BF16)
            upd = jnp.dot(kdec, v, preferred_element_type=F32)
            s_ref[ch] = st * cd_ref[ch][0:1, 0:1] + upd


def _retention(p, ret_decay, *, n_lat):
    t = p.shape[0]
    c = RET_CHUNK
    nc = t // c
    n_lat_c = n_lat // c
    qw, vw = B_HEADS * B_K_DIM, B_HEADS * B_V_DIM
    q_col = 4 * A_WIDTH // qw
    k_col = q_col + 1
    v_col = (4 * A_WIDTH + 2 * qw) // vw
    fwd = lambda n: (n + n_lat_c) % nc
    bwd = lambda n: nc - 1 - n
    return pl.pallas_call(
        _ret_kernel,
        grid=(nc,),
        in_specs=[
            pl.BlockSpec((2 * B_HEADS, 1), lambda n: (0, 0)),
            pl.BlockSpec((c, qw), lambda n: (fwd(n), q_col)),
            pl.BlockSpec((c, qw), lambda n: (fwd(n), k_col)),
            pl.BlockSpec((c, vw), lambda n: (fwd(n), v_col)),
            pl.BlockSpec((c, qw), lambda n: (bwd(n), q_col)),
            pl.BlockSpec((c, qw), lambda n: (bwd(n), k_col)),
            pl.BlockSpec((c, vw), lambda n: (bwd(n), v_col)),
        ],
        out_specs=[
            pl.BlockSpec((c, vw), lambda n: (fwd(n), 0)),
            pl.BlockSpec((c, vw), lambda n: (bwd(n), 0)),
        ],
        out_shape=[jax.ShapeDtypeStruct((t, vw), F32), jax.ShapeDtypeStruct((t, vw), F32)],
        scratch_shapes=[
            pltpu.VMEM((2 * B_HEADS, B_K_DIM, B_V_DIM), F32),
            pltpu.VMEM((2 * B_HEADS, c, c), F32),
            pltpu.VMEM((2 * B_HEADS, c, B_K_DIM), F32),
            pltpu.VMEM((2 * B_HEADS, c, B_K_DIM), F32),
            pltpu.VMEM((2 * B_HEADS, 8, LANES), F32),
        ],
        compiler_params=_cparams(("arbitrary",)),
        name="retention",
    )(ret_decay.reshape(2 * B_HEADS, 1), p, p, p, p, p, p)


def _out_tail(x_ref, y, mod_ref, fin_ref, o_ref, *, n_lat, tm, final):
    d = x_ref.shape[1]
    row = pl.program_id(0) * tm + lax.broadcasted_iota(jnp.int32, (tm, 1), 0)
    gate = jnp.where(row >= n_lat, mod_ref[1:2, 2 * d:3 * d], mod_ref[0:1, 2 * d:3 * d])
    xn = x_ref[...] + gate * y
    if final:
        ms = jnp.mean(xn * xn, axis=-1, keepdims=True)
        xn = xn * lax.rsqrt(ms + EPS) * fin_ref[...]
    o_ref[...] = xn


def _out_even_kernel(x_ref, ya_ref, of_ref, ob_ref, bg_ref, gn_ref, w_ref, mod_ref, fin_ref,
                     o_ref, *, n_lat, tm, final):
    ob = of_ref[...] + ob_ref[...]
    gate = _silu(bg_ref[...].astype(F32))
    parts = []
    for h in range(B_HEADS):
        sl = slice(h * B_V_DIM, (h + 1) * B_V_DIM)
        z = ob[:, sl]
        zc = z - jnp.mean(z, axis=-1, keepdims=True)
        yh = zc * lax.rsqrt(jnp.mean(zc * zc, axis=-1, keepdims=True) + EPS) * gn_ref[:, sl]
        parts.append((yh * gate[:, sl]).astype(BF16))
    yb = jnp.concatenate(parts, axis=1)
    ka = ya_ref.shape[1]
    y = (jnp.dot(ya_ref[...], w_ref[0:ka, :], preferred_element_type=F32)
         + jnp.dot(yb, w_ref[ka:, :], preferred_element_type=F32))
    _out_tail(x_ref, y, mod_ref, fin_ref, o_ref, n_lat=n_lat, tm=tm, final=final)


def _out_odd_kernel(x_ref, y_ref, w_ref, mod_ref, fin_ref, o_ref, *, n_lat, tm, final):
    y = jnp.dot(y_ref[...], w_ref[...], preferred_element_type=F32)
    _out_tail(x_ref, y, mod_ref, fin_ref, o_ref, n_lat=n_lat, tm=tm, final=final)


def _out_proj(xs, mixer_args, mixer_specs, w_out, mod_l, final_norm, *, n_lat, n_rows, even,
              final):
    t, d = xs.shape
    tm = OUT_TM
    body = _out_even_kernel if even else _out_odd_kernel
    kern = functools.partial(body, n_lat=n_lat, tm=tm, final=final)
    row_spec = pl.BlockSpec((tm, d), lambda i: (i, 0))
    const = lambda shape: pl.BlockSpec(shape, lambda i: (0, 0))
    return pl.pallas_call(
        kern,
        grid=(n_rows // tm,),
        in_specs=[row_spec] + mixer_specs + [const(w_out.shape), const(mod_l.shape), const((1, d))],
        out_specs=row_spec,
        out_shape=jax.ShapeDtypeStruct((n_rows, d), F32),
        compiler_params=_cparams(("parallel",)),
        name="out_proj",
    )(xs, *mixer_args, w_out, mod_l, final_norm.reshape(1, d))


def _rope_tables(n_lat, head_dim):
    pos = jnp.arange(n_lat, dtype=jnp.int32)
    rows = (pos // GRID_W).astype(F32)[:, None]
    cols = (pos % GRID_W).astype(F32)[:, None]
    axis_dim = head_dim // 2
    inv = ROPE_THETA ** (-jnp.arange(0, axis_dim, 2, dtype=F32) / axis_dim)
    ar, ac = rows * inv, cols * inv
    cos = jnp.concatenate([jnp.cos(ar), jnp.cos(ar), jnp.cos(ac), jnp.cos(ac)], axis=-1)
    sin = jnp.concatenate([-jnp.sin(ar), jnp.sin(ar), -jnp.sin(ac), jnp.sin(ac)], axis=-1)
    reps = LANES // head_dim
    cos, sin = jnp.tile(cos, (1, reps)), jnp.tile(sin, (1, reps))
    cos = jnp.concatenate([cos, jnp.ones((CTX_LEN, LANES), F32)], axis=0)
    sin = jnp.concatenate([sin, jnp.zeros((CTX_LEN, LANES), F32)], axis=0)
    return cos, sin


def kernel(x, c, ctx, c_ctx, norm_g, w_mod, b_mod, ev_w_in, ev_w_out, diff_lambda, diff_subln,
           ret_decay, ret_gn, od_w_in, od_w_out, qk_norm, final_norm):
    assert x.shape[0] == 1 and ctx.shape[1] == CTX_LEN and x.shape[2] == D_MODEL
    n_lat = x.shape[1]
    t = n_lat + CTX_LEN
    d = D_MODEL
    assert t % PROJ_TM == 0 and t % KV_CHUNK == 0 and n_lat % (ATT_ROWS // 2) == 0

    xs = jnp.concatenate([x[0], ctx[0]], axis=0)
    s_in = jnp.zeros((8, d), F32).at[0].set(c[0]).at[1].set(c_ctx)
    mod = _modulation(s_in, w_mod, b_mod)
    rope_a = _rope_tables(n_lat, A_HEAD_DIM)
    rope_c = _rope_tables(n_lat, C_HEAD_DIM)
    ev_w_in_b, ev_w_out_b = ev_w_in.astype(BF16), ev_w_out.astype(BF16)
    od_w_in_b, od_w_out_b = od_w_in.astype(BF16), od_w_out.astype(BF16)

    for i in range(DEPTH):
        final = i == DEPTH - 1
        n_rows = n_lat if final else t
        tm = OUT_TM
        if i % 2 == 0:
            e = i // 2
            lam_init = 0.8 - 0.6 * math.exp(-0.3 * i)
            p = _proj(xs, mod[i], norm_g[i], ev_w_in_b[e], *rope_a, qk_norm[0], n_lat=n_lat,
                      even=True)
            ya = _diff_attention(p, diff_lambda[e], diff_subln[e], None, lam_init=lam_init,
                                 q_row0=0, n_q=n_lat, k_row0=0, n_k=t)
            if not final:
                ya = _diff_attention(p, diff_lambda[e], diff_subln[e], ya, lam_init=lam_init,
                                     q_row0=n_lat, n_q=CTX_LEN, k_row0=n_lat, n_k=CTX_LEN)
            o_f, o_b = _retention(p, ret_decay[e], n_lat=n_lat)
            bw = B_WIDTH
            args = [ya, o_f, o_b, p, ret_gn[e].reshape(1, bw)]
            specs = [pl.BlockSpec((tm, A_WIDTH), lambda r: (r, 0)),
                     pl.BlockSpec((tm, bw), lambda r: (r, 0)),
                     pl.BlockSpec((tm, bw), lambda r: (r, 0)),
                     pl.BlockSpec((tm, bw), lambda r: (r, (EV_IN - bw) // bw)),
                     pl.BlockSpec((1, bw), lambda r: (0, 0))]
            xs = _out_proj(xs, args, specs, ev_w_out_b[e], mod[i], final_norm, n_lat=n_lat,
                           n_rows=n_rows, even=True, final=final)
        else:
            o = i // 2
            p = _proj(xs, mod[i], norm_g[i], od_w_in_b[o], *rope_c, qk_norm[o], n_lat=n_lat,
                      even=False)
            y = _gqa_attention(p, None, q_row0=0, n_q=n_lat, k_row0=0, n_k=t)
            if not final:
                y = _gqa_attention(p, y, q_row0=n_lat, n_q=CTX_LEN, k_row0=n_lat, n_k=CTX_LEN)
            specs = [pl.BlockSpec((tm, d), lambda r: (r, 0))]
            xs = _out_proj(xs, [y], specs, od_w_out_b[o], mod[i], final_norm, n_lat=n_lat,
                           n_rows=n_rows, even=False, final=final)
    return xs[None]
```

```python
import functools
import math

import jax
import jax.numpy as jnp
from jax import lax
from jax.experimental import pallas as pl
from jax.experimental.pallas import tpu as pltpu

F32 = jnp.float32
BF16 = jnp.bfloat16

D_MODEL = 2048
DEPTH = 4
GRID_W = 64
CTX_LEN = 256
RET_CHUNK = 128
ROPE_THETA = 10000.0
EPS = 1e-6

A_WIDTH = D_MODEL // 2
A_HEADS = 8
A_HEAD_DIM = A_WIDTH // A_HEADS // 2
A_V_DIM = 2 * A_HEAD_DIM
B_WIDTH = D_MODEL - A_WIDTH
B_HEADS = 4
B_V_DIM = B_WIDTH // B_HEADS
B_K_DIM = B_V_DIM // 2
EV_IN = 7168

C_HEADS = 16
C_KV_HEADS = 4
C_GROUP = C_HEADS // C_KV_HEADS
C_HEAD_DIM = D_MODEL // C_HEADS
OD_IN = 5120

LANES = 128
SUBLANES = 8
MXU_TILE = 256
SOFTMAX_ROWS = 16
MAX_CHAINS = 4
LOG2E = math.log2(math.e)
VMEM_LIMIT_BYTES = 56 * 1024 * 1024
PROJ_TM = 1280
PROJ_TN = 512
NORM_ROWS = 256
OUT_TM = 256
KV_CHUNK = 1280
ATT_ROWS = 1024


def _silu(x):
    return x / (1.0 + jnp.exp(-x))


def _cparams(sem):
    return pltpu.CompilerParams(dimension_semantics=sem, vmem_limit_bytes=VMEM_LIMIT_BYTES)


def _mod_kernel(s_ref, w_ref, b_ref, o_ref):
    s = _silu(s_ref[...])
    acc = jnp.dot(s.astype(BF16), w_ref[0].astype(BF16), preferred_element_type=F32)
    o_ref[0] = acc + b_ref[0]


def _modulation(s_in, w_mod, b_mod):
    depth, d, n = w_mod.shape
    tn = 768
    return pl.pallas_call(
        _mod_kernel,
        grid=(depth, n // tn),
        in_specs=[
            pl.BlockSpec((8, d), lambda l, j: (0, 0)),
            pl.BlockSpec((1, d, tn), lambda l, j: (l, 0, j)),
            pl.BlockSpec((1, 1, tn), lambda l, j: (l, 0, j)),
        ],
        out_specs=pl.BlockSpec((1, 8, tn), lambda l, j: (l, 0, j)),
        out_shape=jax.ShapeDtypeStruct((depth, 8, n), F32),
        compiler_params=_cparams(("parallel", "parallel")),
        name="modulation",
    )(s_in, w_mod, b_mod.reshape(depth, 1, n))


def _rope(x, cos, sin, quarter):
    lane = lax.broadcasted_iota(jnp.int32, x.shape, 1)
    first = (lane & quarter) == 0
    partner = jnp.where(first, pltpu.roll(x, LANES - quarter, 1), pltpu.roll(x, quarter, 1))
    return x * cos + partner * sin


def _proj_kernel(x_ref, mod_ref, g_ref, w_ref, cos_ref, sin_ref, qk_ref, o_ref, h_ref, *,
                 n_lat, tm, tn, even):
    i = pl.program_id(0)
    j = pl.program_id(1)
    d = x_ref.shape[1]

    @pl.when(j == 0)
    def _():
        def norm_rows(r, carry):
            r0 = pl.multiple_of(r * NORM_ROWS, NORM_ROWS)
            x = x_ref[pl.ds(r0, NORM_ROWS), :]
            ms = jnp.mean(x * x, axis=-1, keepdims=True)
            y = x * lax.rsqrt(ms + EPS) * g_ref[...]
            row = i * tm + r0 + lax.broadcasted_iota(jnp.int32, (NORM_ROWS, 1), 0)
            is_ctx = row >= n_lat
            shift = jnp.where(is_ctx, mod_ref[1:2, 0:d], mod_ref[0:1, 0:d])
            scale = jnp.where(is_ctx, mod_ref[1:2, d:2 * d], mod_ref[0:1, d:2 * d])
            h_ref[pl.ds(r0, NORM_ROWS), :] = (y * (1.0 + scale) + shift).astype(BF16)
            return carry

        lax.fori_loop(0, tm // NORM_ROWS, norm_rows, 0)

    acc = jnp.dot(h_ref[...], w_ref[...], preferred_element_type=F32)
    groups = tn // LANES

    if even:
        n_rope = 2 * (2 * A_HEADS * A_HEAD_DIM) // tn
        n_q = n_rope // 2
        bk_tile = (4 * A_WIDTH + B_HEADS * B_K_DIM) // tn

        @pl.when(j < n_rope)
        def _():
            cos = cos_ref[...]
            sin = sin_ref[...]
            qscale = jnp.where(j < n_q, A_HEAD_DIM ** -0.5 * LOG2E, 1.0).astype(F32)
            for g in range(groups):
                sl = slice(g * LANES, (g + 1) * LANES)
                o_ref[:, sl] = (_rope(acc[:, sl], cos, sin, A_HEAD_DIM // 4) * qscale).astype(BF16)

        @pl.when(j == bk_tile)
        def _():
            o_ref[...] = (acc * (B_K_DIM ** -0.5)).astype(BF16)

        @pl.when(jnp.logical_and(j >= n_rope, j != bk_tile))
        def _():
            o_ref[...] = acc.astype(BF16)
    else:
        n_q = (C_HEADS * C_HEAD_DIM) // tn
        n_qk = n_q + (C_KV_HEADS * C_HEAD_DIM) // tn

        @pl.when(j < n_qk)
        def _():
            cos = cos_ref[...]
            sin = sin_ref[...]
            is_q = j < n_q
            gain = jnp.where(is_q, qk_ref[0:1, :], qk_ref[1:2, :])
            qscale = jnp.where(is_q, C_HEAD_DIM ** -0.5 * LOG2E, 1.0).astype(F32)
            for g in range(groups):
                sl = slice(g * LANES, (g + 1) * LANES)
                xg = acc[:, sl]
                ms = jnp.mean(xg * xg, axis=-1, keepdims=True)
                yg = xg * lax.rsqrt(ms + EPS) * gain
                o_ref[:, sl] = (_rope(yg, cos, sin, C_HEAD_DIM // 4) * qscale).astype(BF16)

        @pl.when(j >= n_qk)
        def _():
            o_ref[...] = acc.astype(BF16)


def _proj(xs, mod_l, norm_g, w_in, cos, sin, qk_g, *, n_lat, even):
    t, d = xs.shape
    n = w_in.shape[1]
    tm, tn = PROJ_TM, PROJ_TN
    kern = functools.partial(_proj_kernel, n_lat=n_lat, tm=tm, tn=tn, even=even)
    return pl.pallas_call(
        kern,
        grid=(t // tm, n // tn),
        in_specs=[
            pl.BlockSpec((tm, d), lambda i, j: (i, 0)),
            pl.BlockSpec(mod_l.shape, lambda i, j: (0, 0)),
            pl.BlockSpec((1, d), lambda i, j: (0, 0)),
            pl.BlockSpec((d, tn), lambda i, j: (0, j)),
            pl.BlockSpec((tm, LANES), lambda i, j: (i, 0)),
            pl.BlockSpec((tm, LANES), lambda i, j: (i, 0)),
            pl.BlockSpec(qk_g.shape, lambda i, j: (0, 0)),
        ],
        out_specs=pl.BlockSpec((tm, tn), lambda i, j: (i, j)),
        out_shape=jax.ShapeDtypeStruct((t, n), BF16),
        scratch_shapes=[pltpu.VMEM((tm, d), BF16)],
        compiler_params=_cparams(("parallel", "arbitrary")),
        name="proj_even" if even else "proj_odd",
    )(xs, mod_l, norm_g.reshape(1, d), w_in, cos, sin, qk_g)


def _flash_t(qs, k_ref, vt_ref, s_scr, mx_scr, m_scr, l_scr, acc_scr, ck, n_chunks):
    sub = SUBLANES
    m_rows = qs.shape[0]
    n_col = m_rows // MXU_TILE
    n_kt = ck // MXU_TILE

    def scores(c, slot, nt):
        off = pl.multiple_of(c * ck, ck)
        cols = slice(nt * MXU_TILE, (nt + 1) * MXU_TILE)
        s = lax.dot_general(k_ref[pl.ds(off, ck), :], qs[cols], (((1,), (1,)), ((), ())),
                            preferred_element_type=F32)
        s_scr[slot][:, cols] = s
        parts = [s[t * sub:(t + 1) * sub] for t in range(MAX_CHAINS)]
        for r in range(MAX_CHAINS, ck // sub):
            parts[r % MAX_CHAINS] = jnp.maximum(parts[r % MAX_CHAINS], s[r * sub:(r + 1) * sub])
        mx_scr[slot][:, cols] = functools.reduce(jnp.maximum, parts)

    def softmax_pv(c, slot, nt):
        cols = slice(nt * MXU_TILE, (nt + 1) * MXU_TILE)
        m_prev = m_scr[:, cols]
        m_new = jnp.maximum(m_prev, jnp.max(mx_scr[slot][:, cols], axis=0, keepdims=True))
        alpha = jnp.exp2(m_prev - m_new)
        m_scr[:, cols] = m_new
        m_blk = jnp.broadcast_to(m_new, (SOFTMAX_ROWS, MXU_TILE))
        acc = alpha * acc_scr[:, cols]
        l_part = jnp.zeros((sub, MXU_TILE), F32)
        for kt in range(n_kt):
            blocks = []
            for r in range(MXU_TILE // SOFTMAX_ROWS):
                r0 = kt * MXU_TILE + r * SOFTMAX_ROWS
                pb = jnp.exp2(s_scr[slot][r0:r0 + SOFTMAX_ROWS, cols] - m_blk)
                for t in range(SOFTMAX_ROWS // sub):
                    l_part = l_part + pb[t * sub:(t + 1) * sub]
                blocks.append(pb.astype(BF16))
            p_tile = jnp.concatenate(blocks, axis=0)
            acc = acc + jnp.dot(vt_ref[c, :, kt * MXU_TILE:(kt + 1) * MXU_TILE], p_tile,
                                preferred_element_type=F32)
        acc_scr[:, cols] = acc
        l_scr[:, cols] = alpha * l_scr[:, cols] + l_part

    def step(c, slot, with_scores):
        for nt in range(n_col):
            if with_scores:
                scores(c + 1, 1 - slot, nt)
            softmax_pv(c, slot, nt)

    m_scr[...] = jnp.full(m_scr.shape, -jnp.inf, F32)
    l_scr[...] = jnp.zeros(l_scr.shape, F32)
    acc_scr[...] = jnp.zeros(acc_scr.shape, F32)
    for nt in range(n_col):
        scores(0, 0, nt)

    def pair(i, carry):
        step(2 * i, 0, True)
        step(2 * i + 1, 1, True)
        return carry

    n_pairs = (n_chunks - 1) // 2
    lax.fori_loop(0, n_pairs, pair, 0)
    for c in range(2 * n_pairs, n_chunks):
        step(c, c % 2, c + 1 < n_chunks)
    return acc_scr[...], jnp.sum(l_scr[...], axis=0, keepdims=True)


def _flash_scratch(m_rows, ck, dv):
    two = lambda shape, dtype: [pltpu.VMEM(shape, dtype), pltpu.VMEM(shape, dtype)]
    return [two((ck, m_rows), F32), two((SUBLANES, m_rows), F32), pltpu.VMEM((1, m_rows), F32),
            pltpu.VMEM((SUBLANES, m_rows), F32), pltpu.VMEM((dv, m_rows), F32)]


def _v_transposed(p, col0, heads, dv, row0, n_k, ck):
    v = p[row0:row0 + n_k, col0:col0 + heads * dv]
    return v.reshape(n_k // ck, ck, heads, dv).transpose(2, 0, 3, 1)


def _gqa_kernel(q_ref, k_ref, vt_ref, g_ref, o_ref, *scratch, tq, ck, n_chunks):
    hd = C_HEAD_DIM
    qs = jnp.concatenate([q_ref[:, h * hd:(h + 1) * hd] for h in range(C_GROUP)], axis=0)
    acc, l = _flash_t(qs, k_ref, vt_ref, *scratch, ck, n_chunks)
    o_t = acc / l
    for h in range(C_GROUP):
        gate = g_ref[:, h * hd:(h + 1) * hd].astype(F32)
        o = o_t[:, h * tq:(h + 1) * tq].T
        o_ref[:, h * hd:(h + 1) * hd] = (o * _silu(gate)).astype(BF16)


def _skip_first_input(kern, *refs):
    return kern(*refs[1:])


def _gqa_attention(p, y_prev, *, q_row0, n_q, k_row0, n_k):
    t = p.shape[0]
    hd = C_HEAD_DIM
    tq = min(ATT_ROWS // C_GROUP, n_q)
    ck = min(KV_CHUNK, n_k)
    n_chunks = n_k // ck
    qb0, kb0 = q_row0 // tq, k_row0 // n_k
    w = C_GROUP * hd
    k_col0 = C_HEADS
    v_col0 = (C_HEADS + C_KV_HEADS) * hd
    g_col0 = (C_HEADS + 2 * C_KV_HEADS) * hd // w
    vt = _v_transposed(p, v_col0, C_KV_HEADS, hd, k_row0, n_k, ck)
    kern = functools.partial(_gqa_kernel, tq=tq, ck=ck, n_chunks=n_chunks)
    in_specs = [
        pl.BlockSpec((tq, w), lambda h, i: (qb0 + i, h)),
        pl.BlockSpec((n_k, hd), lambda h, i: (kb0, k_col0 + h)),
        pl.BlockSpec((None, n_chunks, hd, ck), lambda h, i: (h, 0, 0, 0)),
        pl.BlockSpec((tq, w), lambda h, i: (qb0 + i, g_col0 + h)),
    ]
    args = [p, p, vt, p]
    aliases = {}
    if y_prev is not None:
        in_specs.insert(0, pl.BlockSpec(memory_space=pl.ANY))
        args.insert(0, y_prev)
        aliases = {0: 0}
        kern = functools.partial(_skip_first_input, kern)
    return pl.pallas_call(
        kern,
        grid=(C_KV_HEADS, n_q // tq),
        in_specs=in_specs,
        out_specs=pl.BlockSpec((tq, w), lambda h, i: (qb0 + i, h)),
        out_shape=jax.ShapeDtypeStruct((t, D_MODEL), BF16),
        scratch_shapes=_flash_scratch(C_GROUP * tq, ck, hd),
        input_output_aliases=aliases,
        compiler_params=_cparams(("parallel", "arbitrary")),
        name="gqa_attention",
    )(*args)


def _diff_kernel(lam_ref, sub_ref, q_ref, k_ref, vt_ref, g_ref, o_ref, *scratch, tq, ck,
                 n_chunks, lam_init):
    q = q_ref[...]
    lane = lax.broadcasted_iota(jnp.int32, q.shape, 1)
    zero = jnp.zeros_like(q)
    qs = jnp.concatenate([jnp.where(lane < A_HEAD_DIM, q, zero),
                          jnp.where(lane >= A_HEAD_DIM, q, zero)], axis=0)
    acc, l = _flash_t(qs, k_ref, vt_ref, *scratch, ck, n_chunks)
    o_t = acc / l
    lp = lam_ref[...]
    lam = (jnp.exp(jnp.sum(lp[0:1] * lp[1:2], axis=-1, keepdims=True))
           - jnp.exp(jnp.sum(lp[2:3] * lp[3:4], axis=-1, keepdims=True)) + lam_init)
    od = (o_t[:, :tq] - lam * o_t[:, tq:]).T
    ms = jnp.mean(od * od, axis=-1, keepdims=True)
    y = od * lax.rsqrt(ms + EPS) * sub_ref[...] * (1.0 - lam_init)
    o_ref[...] = (y * _silu(g_ref[...].astype(F32))).astype(BF16)


def _diff_attention(p, lam_p, subln, y_prev, *, lam_init, q_row0, n_q, k_row0, n_k):
    t = p.shape[0]
    tq = min(ATT_ROWS // 2, n_q)
    ck = min(KV_CHUNK, n_k)
    n_chunks = n_k // ck
    qb0, kb0 = q_row0 // tq, k_row0 // n_k
    w = A_V_DIM
    vt = _v_transposed(p, 2 * A_HEADS * w, A_HEADS, w, k_row0, n_k, ck)
    kern = functools.partial(_diff_kernel, tq=tq, ck=ck, n_chunks=n_chunks, lam_init=lam_init)
    in_specs = [
        pl.BlockSpec(lam_p.shape, lambda h, i: (0, 0)),
        pl.BlockSpec((1, w), lambda h, i: (0, 0)),
        pl.BlockSpec((tq, w), lambda h, i: (qb0 + i, h)),
        pl.BlockSpec((n_k, w), lambda h, i: (kb0, A_HEADS + h)),
        pl.BlockSpec((None, n_chunks, w, ck), lambda h, i: (h, 0, 0, 0)),
        pl.BlockSpec((tq, w), lambda h, i: (qb0 + i, 3 * A_HEADS + h)),
    ]
    args = [lam_p, subln.reshape(1, w), p, p, vt, p]
    aliases = {}
    if y_prev is not None:
        in_specs.insert(0, pl.BlockSpec(memory_space=pl.ANY))
        args.insert(0, y_prev)
        aliases = {0: 0}
        kern = functools.partial(_skip_first_input, kern)
    return pl.pallas_call(
        kern,
        grid=(A_HEADS, n_q // tq),
        in_specs=in_specs,
        out_specs=pl.BlockSpec((tq, w), lambda h, i: (qb0 + i, h)),
        out_shape=jax.ShapeDtypeStruct((t, A_WIDTH), BF16),
        scratch_shapes=_flash_scratch(2 * tq, ck, w),
        input_output_aliases=aliases,
        compiler_params=_cparams(("parallel", "arbitrary")),
        name="diff_attention",
    )(*args)


def _ret_kernel(a_ref, qf_ref, kf_ref, vf_ref, qb_ref, kb_ref, vb_ref, of_ref, ob_ref,
                s_ref, dec_ref, qd_ref, kd_ref, cd_ref):
    n = pl.program_id(0)
    c = RET_CHUNK

    @pl.when(n == 0)
    def _():
        i = lax.broadcasted_iota(jnp.int32, (c, c), 0).astype(F32)
        j = lax.broadcasted_iota(jnp.int32, (c, c), 1).astype(F32)
        for d in range(2):
            for h in range(B_HEADS):
                ch = d * B_HEADS + h
                lg = -jnp.exp(a_ref[ch:ch + 1, :])
                rel = (i - j) if d == 0 else (j - i)
                dec_ref[ch] = jnp.where(rel >= 0, jnp.exp(lg * jnp.maximum(rel, 0.0)), 0.0)
                qd_ref[ch] = jnp.exp(lg * ((i + 1.0) if d == 0 else (c - i)))
                kd_ref[ch] = jnp.exp(lg * ((c - 1.0 - i) if d == 0 else i))
                cd_ref[ch] = jnp.exp(lg * float(c)) + jnp.zeros((8, LANES), F32)
                s_ref[ch] = jnp.zeros(s_ref.shape[1:], F32)

    for d, (q_ref, k_ref, v_ref, o_ref) in enumerate(((qf_ref, kf_ref, vf_ref, of_ref),
                                                      (qb_ref, kb_ref, vb_ref, ob_ref))):
        for h in range(B_HEADS):
            ch = d * B_HEADS + h
            q = q_ref[:, h * B_K_DIM:(h + 1) * B_K_DIM]
            k = k_ref[:, h * B_K_DIM:(h + 1) * B_K_DIM]
            v = v_ref[:, h * B_V_DIM:(h + 1) * B_V_DIM]
            st = s_ref[ch]
            sc = lax.dot_general(q, k, (((1,), (1,)), ((), ())),
                                 preferred_element_type=F32) * dec_ref[ch]
            intra = jnp.dot(sc.astype(BF16), v, preferred_element_type=F32)
            qdec = (q.astype(F32) * qd_ref[ch]).astype(BF16)
            cross = jnp.dot(qdec, st.astype(BF16), preferred_element_type=F32)
            o_ref[:, h * B_V_DIM:(h + 1) * B_V_DIM] = intra + cross
            kdec = (k.astype(F32) * kd_ref[ch]).T.astype(BF16)
            upd = jnp.dot(kdec, v, preferred_element_type=F32)
            s_ref[ch] = st * cd_ref[ch][0:1, 0:1] + upd


def _retention(p, ret_decay, *, n_lat):
    t = p.shape[0]
    c = RET_CHUNK
    nc = t // c
    n_lat_c = n_lat // c
    qw, vw = B_HEADS * B_K_DIM, B_HEADS * B_V_DIM
    q_col = 4 * A_WIDTH // qw
    k_col = q_col + 1
    v_col = (4 * A_WIDTH + 2 * qw) // vw
    fwd = lambda n: (n + n_lat_c) % nc
    bwd = lambda n: nc - 1 - n
    return pl.pallas_call(
        _ret_kernel,
        grid=(nc,),
        in_specs=[
            pl.BlockSpec((2 * B_HEADS, 1), lambda n: (0, 0)),
            pl.BlockSpec((c, qw), lambda n: (fwd(n), q_col)),
            pl.BlockSpec((c, qw), lambda n: (fwd(n), k_col)),
            pl.BlockSpec((c, vw), lambda n: (fwd(n), v_col)),
            pl.BlockSpec((c, qw), lambda n: (bwd(n), q_col)),
            pl.BlockSpec((c, qw), lambda n: (bwd(n), k_col)),
            pl.BlockSpec((c, vw), lambda n: (bwd(n), v_col)),
        ],
        out_specs=[
            pl.BlockSpec((c, vw), lambda n: (fwd(n), 0)),
            pl.BlockSpec((c, vw), lambda n: (bwd(n), 0)),
        ],
        out_shape=[jax.ShapeDtypeStruct((t, vw), F32), jax.ShapeDtypeStruct((t, vw), F32)],
        scratch_shapes=[
            pltpu.VMEM((2 * B_HEADS, B_K_DIM, B_V_DIM), F32),
            pltpu.VMEM((2 * B_HEADS, c, c), F32),
            pltpu.VMEM((2 * B_HEADS, c, B_K_DIM), F32),
            pltpu.VMEM((2 * B_HEADS, c, B_K_DIM), F32),
            pltpu.VMEM((2 * B_HEADS, 8, LANES), F32),
        ],
        compiler_params=_cparams(("arbitrary",)),
        name="retention",
    )(ret_decay.reshape(2 * B_HEADS, 1), p, p, p, p, p, p)


def _out_tail(x_ref, y, mod_ref, fin_ref, o_ref, *, n_lat, tm, final):
    d = x_ref.shape[1]
    row = pl.program_id(0) * tm + lax.broadcasted_iota(jnp.int32, (tm, 1), 0)
    gate = jnp.where(row >= n_lat, mod_ref[1:2, 2 * d:3 * d], mod_ref[0:1, 2 * d:3 * d])
    xn = x_ref[...] + gate * y
    if final:
        ms = jnp.mean(xn * xn, axis=-1, keepdims=True)
        xn = xn * lax.rsqrt(ms + EPS) * fin_ref[...]
    o_ref[...] = xn


def _out_even_kernel(x_ref, ya_ref, of_ref, ob_ref, bg_ref, gn_ref, w_ref, mod_ref, fin_ref,
                     o_ref, *, n_lat, tm, final):
    ob = of_ref[...] + ob_ref[...]
    gate = _silu(bg_ref[...].astype(F32))
    parts = []
    for h in range(B_HEADS):
        sl = slice(h * B_V_DIM, (h + 1) * B_V_DIM)
        z = ob[:, sl]
        zc = z - jnp.mean(z, axis=-1, keepdims=True)
        yh = zc * lax.rsqrt(jnp.mean(zc * zc, axis=-1, keepdims=True) + EPS) * gn_ref[:, sl]
        parts.append((yh * gate[:, sl]).astype(BF16))
    yb = jnp.concatenate(parts, axis=1)
    ka = ya_ref.shape[1]
    y = (jnp.dot(ya_ref[...], w_ref[0:ka, :], preferred_element_type=F32)
         + jnp.dot(yb, w_ref[ka:, :], preferred_element_type=F32))
    _out_tail(x_ref, y, mod_ref, fin_ref, o_ref, n_lat=n_lat, tm=tm, final=final)


def _out_odd_kernel(x_ref, y_ref, w_ref, mod_ref, fin_ref, o_ref, *, n_lat, tm, final):
    y = jnp.dot(y_ref[...], w_ref[...], preferred_element_type=F32)
    _out_tail(x_ref, y, mod_ref, fin_ref, o_ref, n_lat=n_lat, tm=tm, final=final)


def _out_proj(xs, mixer_args, mixer_specs, w_out, mod_l, final_norm, *, n_lat, n_rows, even,
              final):
    t, d = xs.shape
    tm = OUT_TM
    body = _out_even_kernel if even else _out_odd_kernel
    kern = functools.partial(body, n_lat=n_lat, tm=tm, final=final)
    row_spec = pl.BlockSpec((tm, d), lambda i: (i, 0))
    const = lambda shape: pl.BlockSpec(shape, lambda i: (0, 0))
    return pl.pallas_call(
        kern,
        grid=(n_rows // tm,),
        in_specs=[row_spec] + mixer_specs + [const(w_out.shape), const(mod_l.shape), const((1, d))],
        out_specs=row_spec,
        out_shape=jax.ShapeDtypeStruct((n_rows, d), F32),
        compiler_params=_cparams(("parallel",)),
        name="out_proj",
    )(xs, *mixer_args, w_out, mod_l, final_norm.reshape(1, d))


def _rope_tables(n_lat, head_dim):
    pos = jnp.arange(n_lat, dtype=jnp.int32)
    rows = (pos // GRID_W).astype(F32)[:, None]
    cols = (pos % GRID_W).astype(F32)[:, None]
    axis_dim = head_dim // 2
    inv = ROPE_THETA ** (-jnp.arange(0, axis_dim, 2, dtype=F32) / axis_dim)
    ar, ac = rows * inv, cols * inv
    cos = jnp.concatenate([jnp.cos(ar), jnp.cos(ar), jnp.cos(ac), jnp.cos(ac)], axis=-1)
    sin = jnp.concatenate([-jnp.sin(ar), jnp.sin(ar), -jnp.sin(ac), jnp.sin(ac)], axis=-1)
    reps = LANES // head_dim
    cos, sin = jnp.tile(cos, (1, reps)), jnp.tile(sin, (1, reps))
    cos = jnp.concatenate([cos, jnp.ones((CTX_LEN, LANES), F32)], axis=0)
    sin = jnp.concatenate([sin, jnp.zeros((CTX_LEN, LANES), F32)], axis=0)
    return cos, sin


def kernel(x, c, ctx, c_ctx, norm_g, w_mod, b_mod, ev_w_in, ev_w_out, diff_lambda, diff_subln,
           ret_decay, ret_gn, od_w_in, od_w_out, qk_norm, final_norm):
    assert x.shape[0] == 1 and ctx.shape[1] == CTX_LEN and x.shape[2] == D_MODEL
    n_lat = x.shape[1]
    t = n_lat + CTX_LEN
    d = D_MODEL
    assert t % PROJ_TM == 0 and t % KV_CHUNK == 0 and n_lat % (ATT_ROWS // 2) == 0

    xs = jnp.concatenate([x[0], ctx[0]], axis=0)
    s_in = jnp.zeros((8, d), F32).at[0].set(c[0]).at[1].set(c_ctx)
    mod = _modulation(s_in, w_mod, b_mod)
    rope_a = _rope_tables(n_lat, A_HEAD_DIM)
    rope_c = _rope_tables(n_lat, C_HEAD_DIM)
    ev_w_in_b, ev_w_out_b = ev_w_in.astype(BF16), ev_w_out.astype(BF16)
    od_w_in_b, od_w_out_b = od_w_in.astype(BF16), od_w_out.astype(BF16)

    for i in range(DEPTH):
        final = i == DEPTH - 1
        n_rows = n_lat if final else t
        tm = OUT_TM
        if i % 2 == 0:
            e = i // 2
            lam_init = 0.8 - 0.6 * math.exp(-0.3 * i)
            p = _proj(xs, mod[i], norm_g[i], ev_w_in_b[e], *rope_a, qk_norm[0], n_lat=n_lat,
                      even=True)
            ya = _diff_attention(p, diff_lambda[e], diff_subln[e], None, lam_init=lam_init,
                                 q_row0=0, n_q=n_lat, k_row0=0, n_k=t)
            if not final:
                ya = _diff_attention(p, diff_lambda[e], diff_subln[e], ya, lam_init=lam_init,
                                     q_row0=n_lat, n_q=CTX_LEN, k_row0=n_lat, n_k=CTX_LEN)
            o_f, o_b = _retention(p, ret_decay[e], n_lat=n_lat)
            bw = B_WIDTH
            args = [ya, o_f, o_b, p, ret_gn[e].reshape(1, bw)]
            specs = [pl.BlockSpec((tm, A_WIDTH), lambda r: (r, 0)),
                     pl.BlockSpec((tm, bw), lambda r: (r, 0)),
                     pl.BlockSpec((tm, bw), lambda r: (r, 0)),
                     pl.BlockSpec((tm, bw), lambda r: (r, (EV_IN - bw) // bw)),
                     pl.BlockSpec((1, bw), lambda r: (0, 0))]
            xs = _out_proj(xs, args, specs, ev_w_out_b[e], mod[i], final_norm, n_lat=n_lat,
                           n_rows=n_rows, even=True, final=final)
        else:
            o = i // 2
            p = _proj(xs, mod[i], norm_g[i], od_w_in_b[o], *rope_c, qk_norm[o], n_lat=n_lat,
                      even=False)
            y = _gqa_attention(p, None, q_row0=0, n_q=n_lat, k_row0=0, n_k=t)
            if not final:
                y = _gqa_attention(p, y, q_row0=n_lat, n_q=CTX_LEN, k_row0=n_lat, n_k=CTX_LEN)
            specs = [pl.BlockSpec((tm, d), lambda r: (r, 0))]
            xs = _out_proj(xs, [y], specs, od_w_out_b[o], mod[i], final_norm, n_lat=n_lat,
                           n_rows=n_rows, even=False, final=final)
    return xs[None]
```

```python
import functools
import math

import jax
import jax.numpy as jnp
from jax import lax
from jax.experimental import pallas as pl
from jax.experimental.pallas import tpu as pltpu

F32 = jnp.float32
BF16 = jnp.bfloat16

D_MODEL = 2048
DEPTH = 4
GRID_W = 64
CTX_LEN = 256
RET_CHUNK = 128
ROPE_THETA = 10000.0
EPS = 1e-6

A_WIDTH = D_MODEL // 2
A_HEADS = 8
A_HEAD_DIM = A_WIDTH // A_HEADS // 2
A_V_DIM = 2 * A_HEAD_DIM
B_WIDTH = D_MODEL - A_WIDTH
B_HEADS = 4
B_V_DIM = B_WIDTH // B_HEADS
B_K_DIM = B_V_DIM // 2
EV_IN = 7168

C_HEADS = 16
C_KV_HEADS = 4
C_GROUP = C_HEADS // C_KV_HEADS
C_HEAD_DIM = D_MODEL // C_HEADS
OD_IN = 5120

LANES = 128
SUBLANES = 8
MXU_TILE = 256
SCORE_ROWS = 256
SOFTMAX_ROWS = 16
MAX_CHAINS = 4
LOG2E = math.log2(math.e)
VMEM_LIMIT_BYTES = 56 * 1024 * 1024
PROJ_TM = 1280
PROJ_TN = 512
NORM_ROWS = 256
OUT_TM = 256
KV_CHUNK = 1280
ATT_ROWS = 1024


def _silu(x):
    return x / (1.0 + jnp.exp(-x))


def _cparams(sem):
    return pltpu.CompilerParams(dimension_semantics=sem, vmem_limit_bytes=VMEM_LIMIT_BYTES)


def _mod_kernel(s_ref, w_ref, b_ref, o_ref):
    s = _silu(s_ref[...])
    acc = jnp.dot(s.astype(BF16), w_ref[0].astype(BF16), preferred_element_type=F32)
    o_ref[0] = acc + b_ref[0]


def _modulation(s_in, w_mod, b_mod):
    depth, d, n = w_mod.shape
    tn = 768
    return pl.pallas_call(
        _mod_kernel,
        grid=(depth, n // tn),
        in_specs=[
            pl.BlockSpec((8, d), lambda l, j: (0, 0)),
            pl.BlockSpec((1, d, tn), lambda l, j: (l, 0, j)),
            pl.BlockSpec((1, 1, tn), lambda l, j: (l, 0, j)),
        ],
        out_specs=pl.BlockSpec((1, 8, tn), lambda l, j: (l, 0, j)),
        out_shape=jax.ShapeDtypeStruct((depth, 8, n), F32),
        compiler_params=_cparams(("parallel", "parallel")),
        name="modulation",
    )(s_in, w_mod, b_mod.reshape(depth, 1, n))


def _rope(x, cos, sin, quarter):
    lane = lax.broadcasted_iota(jnp.int32, x.shape, 1)
    first = (lane & quarter) == 0
    partner = jnp.where(first, pltpu.roll(x, LANES - quarter, 1), pltpu.roll(x, quarter, 1))
    return x * cos + partner * sin


def _proj_kernel(x_ref, mod_ref, g_ref, w_ref, cos_ref, sin_ref, qk_ref, o_ref, h_ref, *,
                 n_lat, tm, tn, even):
    i = pl.program_id(0)
    j = pl.program_id(1)
    d = x_ref.shape[1]

    @pl.when(j == 0)
    def _():
        def norm_rows(r, carry):
            r0 = pl.multiple_of(r * NORM_ROWS, NORM_ROWS)
            x = x_ref[pl.ds(r0, NORM_ROWS), :]
            ms = jnp.mean(x * x, axis=-1, keepdims=True)
            y = x * lax.rsqrt(ms + EPS) * g_ref[...]
            row = i * tm + r0 + lax.broadcasted_iota(jnp.int32, (NORM_ROWS, 1), 0)
            is_ctx = row >= n_lat
            shift = jnp.where(is_ctx, mod_ref[1:2, 0:d], mod_ref[0:1, 0:d])
            scale = jnp.where(is_ctx, mod_ref[1:2, d:2 * d], mod_ref[0:1, d:2 * d])
            h_ref[pl.ds(r0, NORM_ROWS), :] = (y * (1.0 + scale) + shift).astype(BF16)
            return carry

        lax.fori_loop(0, tm // NORM_ROWS, norm_rows, 0)

    acc = jnp.dot(h_ref[...], w_ref[...], preferred_element_type=F32)
    groups = tn // LANES

    if even:
        n_rope = 2 * (2 * A_HEADS * A_HEAD_DIM) // tn
        n_q = n_rope // 2
        bk_tile = (4 * A_WIDTH + B_HEADS * B_K_DIM) // tn

        @pl.when(j < n_rope)
        def _():
            cos = cos_ref[...]
            sin = sin_ref[...]
            qscale = jnp.where(j < n_q, A_HEAD_DIM ** -0.5 * LOG2E, 1.0).astype(F32)
            for g in range(groups):
                sl = slice(g * LANES, (g + 1) * LANES)
                o_ref[:, sl] = (_rope(acc[:, sl], cos, sin, A_HEAD_DIM // 4) * qscale).astype(BF16)

        @pl.when(j == bk_tile)
        def _():
            o_ref[...] = (acc * (B_K_DIM ** -0.5)).astype(BF16)

        @pl.when(jnp.logical_and(j >= n_rope, j != bk_tile))
        def _():
            o_ref[...] = acc.astype(BF16)
    else:
        n_q = (C_HEADS * C_HEAD_DIM) // tn
        n_qk = n_q + (C_KV_HEADS * C_HEAD_DIM) // tn

        @pl.when(j < n_qk)
        def _():
            cos = cos_ref[...]
            sin = sin_ref[...]
            is_q = j < n_q
            gain = jnp.where(is_q, qk_ref[0:1, :], qk_ref[1:2, :])
            qscale = jnp.where(is_q, C_HEAD_DIM ** -0.5 * LOG2E, 1.0).astype(F32)
            for g in range(groups):
                sl = slice(g * LANES, (g + 1) * LANES)
                xg = acc[:, sl]
                ms = jnp.mean(xg * xg, axis=-1, keepdims=True)
                yg = xg * lax.rsqrt(ms + EPS) * gain
                o_ref[:, sl] = (_rope(yg, cos, sin, C_HEAD_DIM // 4) * qscale).astype(BF16)

        @pl.when(j >= n_qk)
        def _():
            o_ref[...] = acc.astype(BF16)


def _proj(xs, mod_l, norm_g, w_in, cos, sin, qk_g, *, n_lat, even):
    t, d = xs.shape
    n = w_in.shape[1]
    tm, tn = PROJ_TM, PROJ_TN
    kern = functools.partial(_proj_kernel, n_lat=n_lat, tm=tm, tn=tn, even=even)
    return pl.pallas_call(
        kern,
        grid=(t // tm, n // tn),
        in_specs=[
            pl.BlockSpec((tm, d), lambda i, j: (i, 0)),
            pl.BlockSpec(mod_l.shape, lambda i, j: (0, 0)),
            pl.BlockSpec((1, d), lambda i, j: (0, 0)),
            pl.BlockSpec((d, tn), lambda i, j: (0, j)),
            pl.BlockSpec((tm, LANES), lambda i, j: (i, 0)),
            pl.BlockSpec((tm, LANES), lambda i, j: (i, 0)),
            pl.BlockSpec(qk_g.shape, lambda i, j: (0, 0)),
        ],
        out_specs=pl.BlockSpec((tm, tn), lambda i, j: (i, j)),
        out_shape=jax.ShapeDtypeStruct((t, n), BF16),
        scratch_shapes=[pltpu.VMEM((tm, d), BF16)],
        compiler_params=_cparams(("parallel", "arbitrary")),
        name="proj_even" if even else "proj_odd",
    )(xs, mod_l, norm_g.reshape(1, d), w_in, cos, sin, qk_g)


def _flash_t(qs, k_ref, vt_ref, s_scr, mx_scr, m_scr, l_scr, acc_scr, ck, n_chunks):
    sub = SUBLANES
    m_rows = qs.shape[0]
    n_col = m_rows // MXU_TILE
    n_kt = ck // MXU_TILE

    def scores(c, slot, nt):
        off = pl.multiple_of(c * ck, ck)
        cols = slice(nt * MXU_TILE, (nt + 1) * MXU_TILE)
        q_t = qs[cols]
        parts = None
        for kb in range(ck // SCORE_ROWS):
            rows = pl.ds(pl.multiple_of(off + kb * SCORE_ROWS, SCORE_ROWS), SCORE_ROWS)
            s = lax.dot_general(k_ref[rows, :], q_t, (((1,), (1,)), ((), ())),
                                preferred_element_type=F32)
            s_scr[slot][kb * SCORE_ROWS:(kb + 1) * SCORE_ROWS, cols] = s
            blk = [s[t * sub:(t + 1) * sub] for t in range(SCORE_ROWS // sub)]
            if parts is None:
                parts = blk[:MAX_CHAINS]
                blk = blk[MAX_CHAINS:]
            for r, b in enumerate(blk):
                parts[r % MAX_CHAINS] = jnp.maximum(parts[r % MAX_CHAINS], b)
        mx_scr[slot][:, cols] = functools.reduce(jnp.maximum, parts)

    def softmax_pv(c, slot, nt):
        cols = slice(nt * MXU_TILE, (nt + 1) * MXU_TILE)
        m_prev = m_scr[:, cols]
        m_new = jnp.maximum(m_prev, jnp.max(mx_scr[slot][:, cols], axis=0, keepdims=True))
        alpha = jnp.exp2(m_prev - m_new)
        m_scr[:, cols] = m_new
        m_blk = jnp.broadcast_to(m_new, (SOFTMAX_ROWS, MXU_TILE))
        acc = alpha * acc_scr[:, cols]
        l_part = jnp.zeros((sub, MXU_TILE), F32)
        blocks = []
        for r in range(ck // SOFTMAX_ROWS):
            r0 = r * SOFTMAX_ROWS
            pb = jnp.exp2(s_scr[slot][r0:r0 + SOFTMAX_ROWS, cols] - m_blk)
            for t in range(SOFTMAX_ROWS // sub):
                l_part = l_part + pb[t * sub:(t + 1) * sub]
            blocks.append(pb.astype(BF16))
        p_col = jnp.concatenate(blocks, axis=0)
        acc = acc + jnp.dot(vt_ref[c], p_col, preferred_element_type=F32)
        acc_scr[:, cols] = acc
        l_scr[:, cols] = alpha * l_scr[:, cols] + l_part

    def step(c, slot, with_scores):
        for nt in range(n_col):
            if with_scores:
                scores(c + 1, 1 - slot, nt)
            softmax_pv(c, slot, nt)

    m_scr[...] = jnp.full(m_scr.shape, -jnp.inf, F32)
    l_scr[...] = jnp.zeros(l_scr.shape, F32)
    acc_scr[...] = jnp.zeros(acc_scr.shape, F32)
    for nt in range(n_col):
        scores(0, 0, nt)

    def one(c, carry):
        lax.cond(c % 2 == 0, lambda: step(c, 0, True), lambda: step(c, 1, True))
        return carry

    n_pairs = (n_chunks - 1) // 2
    lax.fori_loop(0, 2 * n_pairs, one, 0)
    for c in range(2 * n_pairs, n_chunks):
        step(c, c % 2, c + 1 < n_chunks)
    return acc_scr[...], jnp.sum(l_scr[...], axis=0, keepdims=True)


def _flash_scratch(m_rows, ck, dv):
    two = lambda shape, dtype: [pltpu.VMEM(shape, dtype), pltpu.VMEM(shape, dtype)]
    return [two((ck, m_rows), F32), two((SUBLANES, m_rows), F32), pltpu.VMEM((1, m_rows), F32),
            pltpu.VMEM((SUBLANES, m_rows), F32), pltpu.VMEM((dv, m_rows), F32)]


def _v_transposed(p, col0, heads, dv, row0, n_k, ck):
    v = p[row0:row0 + n_k, col0:col0 + heads * dv]
    return v.reshape(n_k // ck, ck, heads, dv).transpose(2, 0, 3, 1)


def _gqa_kernel(q_ref, k_ref, vt_ref, g_ref, o_ref, *scratch, tq, ck, n_chunks):
    hd = C_HEAD_DIM
    qs = jnp.concatenate([q_ref[:, h * hd:(h + 1) * hd] for h in range(C_GROUP)], axis=0)
    acc, l = _flash_t(qs, k_ref, vt_ref, *scratch, ck, n_chunks)
    o_t = acc / l
    for h in range(C_GROUP):
        gate = g_ref[:, h * hd:(h + 1) * hd].astype(F32)
        o = o_t[:, h * tq:(h + 1) * tq].T
        o_ref[:, h * hd:(h + 1) * hd] = (o * _silu(gate)).astype(BF16)


def _skip_first_input(kern, *refs):
    return kern(*refs[1:])


def _gqa_attention(p, y_prev, *, q_row0, n_q, k_row0, n_k):
    t = p.shape[0]
    hd = C_HEAD_DIM
    tq = min(ATT_ROWS // C_GROUP, n_q)
    ck = min(KV_CHUNK, n_k)
    n_chunks = n_k // ck
    qb0, kb0 = q_row0 // tq, k_row0 // n_k
    w = C_GROUP * hd
    k_col0 = C_HEADS
    v_col0 = (C_HEADS + C_KV_HEADS) * hd
    g_col0 = (C_HEADS + 2 * C_KV_HEADS) * hd // w
    vt = _v_transposed(p, v_col0, C_KV_HEADS, hd, k_row0, n_k, ck)
    kern = functools.partial(_gqa_kernel, tq=tq, ck=ck, n_chunks=n_chunks)
    in_specs = [
        pl.BlockSpec((tq, w), lambda h, i: (qb0 + i, h)),
        pl.BlockSpec((n_k, hd), lambda h, i: (kb0, k_col0 + h)),
        pl.BlockSpec((None, n_chunks, hd, ck), lambda h, i: (h, 0, 0, 0)),
        pl.BlockSpec((tq, w), lambda h, i: (qb0 + i, g_col0 + h)),
    ]
    args = [p, p, vt, p]
    aliases = {}
    if y_prev is not None:
        in_specs.insert(0, pl.BlockSpec(memory_space=pl.ANY))
        args.insert(0, y_prev)
        aliases = {0: 0}
        kern = functools.partial(_skip_first_input, kern)
    return pl.pallas_call(
        kern,
        grid=(C_KV_HEADS, n_q // tq),
        in_specs=in_specs,
        out_specs=pl.BlockSpec((tq, w), lambda h, i: (qb0 + i, h)),
        out_shape=jax.ShapeDtypeStruct((t, D_MODEL), BF16),
        scratch_shapes=_flash_scratch(C_GROUP * tq, ck, hd),
        input_output_aliases=aliases,
        compiler_params=_cparams(("parallel", "arbitrary")),
        name="gqa_attention",
    )(*args)


def _diff_kernel(lam_ref, sub_ref, q_ref, k_ref, vt_ref, g_ref, o_ref, *scratch, tq, ck,
                 n_chunks, lam_init):
    q = q_ref[...]
    lane = lax.broadcasted_iota(jnp.int32, q.shape, 1)
    zero = jnp.zeros_like(q)
    qs = jnp.concatenate([jnp.where(lane < A_HEAD_DIM, q, zero),
                          jnp.where(lane >= A_HEAD_DIM, q, zero)], axis=0)
    acc, l = _flash_t(qs, k_ref, vt_ref, *scratch, ck, n_chunks)
    o_t = acc / l
    lp = lam_ref[...]
    lam = (jnp.exp(jnp.sum(lp[0:1] * lp[1:2], axis=-1, keepdims=True))
           - jnp.exp(jnp.sum(lp[2:3] * lp[3:4], axis=-1, keepdims=True)) + lam_init)
    od = (o_t[:, :tq] - lam * o_t[:, tq:]).T
    ms = jnp.mean(od * od, axis=-1, keepdims=True)
    y = od * lax.rsqrt(ms + EPS) * sub_ref[...] * (1.0 - lam_init)
    o_ref[...] = (y * _silu(g_ref[...].astype(F32))).astype(BF16)


def _diff_attention(p, lam_p, subln, y_prev, *, lam_init, q_row0, n_q, k_row0, n_k):
    t = p.shape[0]
    tq = min(ATT_ROWS // 2, n_q)
    ck = min(KV_CHUNK, n_k)
    n_chunks = n_k // ck
    qb0, kb0 = q_row0 // tq, k_row0 // n_k
    w = A_V_DIM
    vt = _v_transposed(p, 2 * A_HEADS * w, A_HEADS, w, k_row0, n_k, ck)
    kern = functools.partial(_diff_kernel, tq=tq, ck=ck, n_chunks=n_chunks, lam_init=lam_init)
    in_specs = [
        pl.BlockSpec(lam_p.shape, lambda h, i: (0, 0)),
        pl.BlockSpec((1, w), lambda h, i: (0, 0)),
        pl.BlockSpec((tq, w), lambda h, i: (qb0 + i, h)),
        pl.BlockSpec((n_k, w), lambda h, i: (kb0, A_HEADS + h)),
        pl.BlockSpec((None, n_chunks, w, ck), lambda h, i: (h, 0, 0, 0)),
        pl.BlockSpec((tq, w), lambda h, i: (qb0 + i, 3 * A_HEADS + h)),
    ]
    args = [lam_p, subln.reshape(1, w), p, p, vt, p]
    aliases = {}
    if y_prev is not None:
        in_specs.insert(0, pl.BlockSpec(memory_space=pl.ANY))
        args.insert(0, y_prev)
        aliases = {0: 0}
        kern = functools.partial(_skip_first_input, kern)
    return pl.pallas_call(
        kern,
        grid=(A_HEADS, n_q // tq),
        in_specs=in_specs,
        out_specs=pl.BlockSpec((tq, w), lambda h, i: (qb0 + i, h)),
        out_shape=jax.ShapeDtypeStruct((t, A_WIDTH), BF16),
        scratch_shapes=_flash_scratch(2 * tq, ck, w),
        input_output_aliases=aliases,
        compiler_params=_cparams(("parallel", "arbitrary")),
        name="diff_attention",
    )(*args)


def _ret_kernel(a_ref, qf_ref, kf_ref, vf_ref, qb_ref, kb_ref, vb_ref, of_ref, ob_ref,
                s_ref, dec_ref, qd_ref, kd_ref, cd_ref):
    n = pl.program_id(0)
    c = RET_CHUNK

    @pl.when(n == 0)
    def _():
        i = lax.broadcasted_iota(jnp.int32, (c, c), 0).astype(F32)
        j = lax.broadcasted_iota(jnp.int32, (c, c), 1).astype(F32)
        for d in range(2):
            for h in range(B_HEADS):
                ch = d * B_HEADS + h
                lg = -jnp.exp(a_ref[ch:ch + 1, :])
                rel = (i - j) if d == 0 else (j - i)
                dec_ref[ch] = jnp.where(rel >= 0, jnp.exp(lg * jnp.maximum(rel, 0.0)), 0.0)
                qd_ref[ch] = jnp.exp(lg * ((i + 1.0) if d == 0 else (c - i)))
                kd_ref[ch] = jnp.exp(lg * ((c - 1.0 - i) if d == 0 else i))
                cd_ref[ch] = jnp.exp(lg * float(c)) + jnp.zeros((8, LANES), F32)
                s_ref[ch] = jnp.zeros(s_ref.shape[1:], F32)

    for d, (q_ref, k_ref, v_ref, o_ref) in enumerate(((qf_ref, kf_ref, vf_ref, of_ref),
                                                      (qb_ref, kb_ref, vb_ref, ob_ref))):
        for h in range(B_HEADS):
            ch = d * B_HEADS + h
            q = q_ref[:, h * B_K_DIM:(h + 1) * B_K_DIM]
            k = k_ref[:, h * B_K_DIM:(h + 1) * B_K_DIM]
            v = v_ref[:, h * B_V_DIM:(h + 1) * B_V_DIM]
            st = s_ref[ch]
            sc = lax.dot_general(q, k, (((1,), (1,)), ((), ())),
                                 preferred_element_type=F32) * dec_ref[ch]
            intra = jnp.dot(sc.astype(BF16), v, preferred_element_type=F32)
            qdec = (q.astype(F32) * qd_ref[ch]).astype(BF16)
            cross = jnp.dot(qdec, st.astype(BF16), preferred_element_type=F32)
            o_ref[:, h * B_V_DIM:(h + 1) * B_V_DIM] = intra + cross
            kdec = (k.astype(F32) * kd_ref[ch]).T.astype(BF16)
            upd = jnp.dot(kdec, v, preferred_element_type=F32)
            s_ref[ch] = st * cd_ref[ch][0:1, 0:1] + upd


def _retention(p, ret_decay, *, n_lat):
    t = p.shape[0]
    c = RET_CHUNK
    nc = t // c
    n_lat_c = n_lat // c
    qw, vw = B_HEADS * B_K_DIM, B_HEADS * B_V_DIM
    q_col = 4 * A_WIDTH // qw
    k_col = q_col + 1
    v_col = (4 * A_WIDTH + 2 * qw) // vw
    fwd = lambda n: (n + n_lat_c) % nc
    bwd = lambda n: nc - 1 - n
    return pl.pallas_call(
        _ret_kernel,
        grid=(nc,),
        in_specs=[
            pl.BlockSpec((2 * B_HEADS, 1), lambda n: (0, 0)),
            pl.BlockSpec((c, qw), lambda n: (fwd(n), q_col)),
            pl.BlockSpec((c, qw), lambda n: (fwd(n), k_col)),
            pl.BlockSpec((c, vw), lambda n: (fwd(n), v_col)),
            pl.BlockSpec((c, qw), lambda n: (bwd(n), q_col)),
            pl.BlockSpec((c, qw), lambda n: (bwd(n), k_col)),
            pl.BlockSpec((c, vw), lambda n: (bwd(n), v_col)),
        ],
        out_specs=[
            pl.BlockSpec((c, vw), lambda n: (fwd(n), 0)),
            pl.BlockSpec((c, vw), lambda n: (bwd(n), 0)),
        ],
        out_shape=[jax.ShapeDtypeStruct((t, vw), F32), jax.ShapeDtypeStruct((t, vw), F32)],
        scratch_shapes=[
            pltpu.VMEM((2 * B_HEADS, B_K_DIM, B_V_DIM), F32),
            pltpu.VMEM((2 * B_HEADS, c, c), F32),
            pltpu.VMEM((2 * B_HEADS, c, B_K_DIM), F32),
            pltpu.VMEM((2 * B_HEADS, c, B_K_DIM), F32),
            pltpu.VMEM((2 * B_HEADS, 8, LANES), F32),
        ],
        compiler_params=_cparams(("arbitrary",)),
        name="retention",
    )(ret_decay.reshape(2 * B_HEADS, 1), p, p, p, p, p, p)


def _out_tail(x_ref, y, mod_ref, fin_ref, o_ref, *, n_lat, tm, final):
    d = x_ref.shape[1]
    row = pl.program_id(0) * tm + lax.broadcasted_iota(jnp.int32, (tm, 1), 0)
    gate = jnp.where(row >= n_lat, mod_ref[1:2, 2 * d:3 * d], mod_ref[0:1, 2 * d:3 * d])
    xn = x_ref[...] + gate * y
    if final:
        ms = jnp.mean(xn * xn, axis=-1, keepdims=True)
        xn = xn * lax.rsqrt(ms + EPS) * fin_ref[...]
    o_ref[...] = xn


def _out_even_kernel(x_ref, ya_ref, of_ref, ob_ref, bg_ref, gn_ref, w_ref, mod_ref, fin_ref,
                     o_ref, *, n_lat, tm, final):
    ob = of_ref[...] + ob_ref[...]
    gate = _silu(bg_ref[...].astype(F32))
    parts = []
    for h in range(B_HEADS):
        sl = slice(h * B_V_DIM, (h + 1) * B_V_DIM)
        z = ob[:, sl]
        zc = z - jnp.mean(z, axis=-1, keepdims=True)
        yh = zc * lax.rsqrt(jnp.mean(zc * zc, axis=-1, keepdims=True) + EPS) * gn_ref[:, sl]
        parts.append((yh * gate[:, sl]).astype(BF16))
    yb = jnp.concatenate(parts, axis=1)
    ka = ya_ref.shape[1]
    y = (jnp.dot(ya_ref[...], w_ref[0:ka, :], preferred_element_type=F32)
         + jnp.dot(yb, w_ref[ka:, :], preferred_element_type=F32))
    _out_tail(x_ref, y, mod_ref, fin_ref, o_ref, n_lat=n_lat, tm=tm, final=final)


def _out_odd_kernel(x_ref, y_ref, w_ref, mod_ref, fin_ref, o_ref, *, n_lat, tm, final):
    y = jnp.dot(y_ref[...], w_ref[...], preferred_element_type=F32)
    _out_tail(x_ref, y, mod_ref, fin_ref, o_ref, n_lat=n_lat, tm=tm, final=final)


def _out_proj(xs, mixer_args, mixer_specs, w_out, mod_l, final_norm, *, n_lat, n_rows, even,
              final):
    t, d = xs.shape
    tm = OUT_TM
    body = _out_even_kernel if even else _out_odd_kernel
    kern = functools.partial(body, n_lat=n_lat, tm=tm, final=final)
    row_spec = pl.BlockSpec((tm, d), lambda i: (i, 0))
    const = lambda shape: pl.BlockSpec(shape, lambda i: (0, 0))
    return pl.pallas_call(
        kern,
        grid=(n_rows // tm,),
        in_specs=[row_spec] + mixer_specs + [const(w_out.shape), const(mod_l.shape), const((1, d))],
        out_specs=row_spec,
        out_shape=jax.ShapeDtypeStruct((n_rows, d), F32),
        compiler_params=_cparams(("parallel",)),
        name="out_proj",
    )(xs, *mixer_args, w_out, mod_l, final_norm.reshape(1, d))


def _rope_tables(n_lat, head_dim):
    pos = jnp.arange(n_lat, dtype=jnp.int32)
    rows = (pos // GRID_W).astype(F32)[:, None]
    cols = (pos % GRID_W).astype(F32)[:, None]
    axis_dim = head_dim // 2
    inv = ROPE_THETA ** (-jnp.arange(0, axis_dim, 2, dtype=F32) / axis_dim)
    ar, ac = rows * inv, cols * inv
    cos = jnp.concatenate([jnp.cos(ar), jnp.cos(ar), jnp.cos(ac), jnp.cos(ac)], axis=-1)
    sin = jnp.concatenate([-jnp.sin(ar), jnp.sin(ar), -jnp.sin(ac), jnp.sin(ac)], axis=-1)
    reps = LANES // head_dim
    cos, sin = jnp.tile(cos, (1, reps)), jnp.tile(sin, (1, reps))
    cos = jnp.concatenate([cos, jnp.ones((CTX_LEN, LANES), F32)], axis=0)
    sin = jnp.concatenate([sin, jnp.zeros((CTX_LEN, LANES), F32)], axis=0)
    return cos, sin


def kernel(x, c, ctx, c_ctx, norm_g, w_mod, b_mod, ev_w_in, ev_w_out, diff_lambda, diff_subln,
           ret_decay, ret_gn, od_w_in, od_w_out, qk_norm, final_norm):
    assert x.shape[0] == 1 and ctx.shape[1] == CTX_LEN and x.shape[2] == D_MODEL
    n_lat = x.shape[1]
    t = n_lat + CTX_LEN
    d = D_MODEL
    assert t % PROJ_TM == 0 and t % KV_CHUNK == 0 and n_lat % (ATT_ROWS // 2) == 0

    xs = jnp.concatenate([x[0], ctx[0]], axis=0)
    s_in = jnp.zeros((8, d), F32).at[0].set(c[0]).at[1].set(c_ctx)
    mod = _modulation(s_in, w_mod, b_mod)
    rope_a = _rope_tables(n_lat, A_HEAD_DIM)
    rope_c = _rope_tables(n_lat, C_HEAD_DIM)
    ev_w_in_b, ev_w_out_b = ev_w_in.astype(BF16), ev_w_out.astype(BF16)
    od_w_in_b, od_w_out_b = od_w_in.astype(BF16), od_w_out.astype(BF16)

    for i in range(DEPTH):
        final = i == DEPTH - 1
        n_rows = n_lat if final else t
        tm = OUT_TM
        if i % 2 == 0:
            e = i // 2
            lam_init = 0.8 - 0.6 * math.exp(-0.3 * i)
            p = _proj(xs, mod[i], norm_g[i], ev_w_in_b[e], *rope_a, qk_norm[0], n_lat=n_lat,
                      even=True)
            ya = _diff_attention(p, diff_lambda[e], diff_subln[e], None, lam_init=lam_init,
                                 q_row0=0, n_q=n_lat, k_row0=0, n_k=t)
            if not final:
                ya = _diff_attention(p, diff_lambda[e], diff_subln[e], ya, lam_init=lam_init,
                                     q_row0=n_lat, n_q=CTX_LEN, k_row0=n_lat, n_k=CTX_LEN)
            o_f, o_b = _retention(p, ret_decay[e], n_lat=n_lat)
            bw = B_WIDTH
            args = [ya, o_f, o_b, p, ret_gn[e].reshape(1, bw)]
            specs = [pl.BlockSpec((tm, A_WIDTH), lambda r: (r, 0)),
                     pl.BlockSpec((tm, bw), lambda r: (r, 0)),
                     pl.BlockSpec((tm, bw), lambda r: (r, 0)),
                     pl.BlockSpec((tm, bw), lambda r: (r, (EV_IN - bw) // bw)),
                     pl.BlockSpec((1, bw), lambda r: (0, 0))]
            xs = _out_proj(xs, args, specs, ev_w_out_b[e], mod[i], final_norm, n_lat=n_lat,
                           n_rows=n_rows, even=True, final=final)
        else:
            o = i // 2
            p = _proj(xs, mod[i], norm_g[i], od_w_in_b[o], *rope_c, qk_norm[o], n_lat=n_lat,
                      even=False)
            y = _gqa_attention(p, None, q_row0=0, n_q=n_lat, k_row0=0, n_k=t)
            if not final:
                y = _gqa_attention(p, y, q_row0=n_lat, n_q=CTX_LEN, k_row0=n_lat, n_k=CTX_LEN)
            specs = [pl.BlockSpec((tm, d), lambda r: (r, 0))]
            xs = _out_proj(xs, [y], specs, od_w_out_b[o], mod[i], final_norm, n_lat=n_lat,
                           n_rows=n_rows, even=False, final=final)
    return xs[None]
```

```python
import functools
import math

import jax
import jax.numpy as jnp
from jax import lax
from jax.experimental import pallas as pl
from jax.experimental.pallas import tpu as pltpu

F32 = jnp.float32
BF16 = jnp.bfloat16

D_MODEL = 2048
DEPTH = 4
GRID_W = 64
CTX_LEN = 256
RET_CHUNK = 128
ROPE_THETA = 10000.0
EPS = 1e-6

A_WIDTH = D_MODEL // 2
A_HEADS = 8
A_HEAD_DIM = A_WIDTH // A_HEADS // 2
A_V_DIM = 2 * A_HEAD_DIM
B_WIDTH = D_MODEL - A_WIDTH
B_HEADS = 4
B_V_DIM = B_WIDTH // B_HEADS
B_K_DIM = B_V_DIM // 2
EV_IN = 7168

C_HEADS = 16
C_KV_HEADS = 4
C_GROUP = C_HEADS // C_KV_HEADS
C_HEAD_DIM = D_MODEL // C_HEADS
OD_IN = 5120

LANES = 128
SUBLANES = 8
MXU_TILE = 256
ONES_ROWS = 16
SCORE_ROWS = 256
SOFTMAX_ROWS = 32
MAX_CHAINS = 4
LOG2E = math.log2(math.e)
VMEM_LIMIT_BYTES = 56 * 1024 * 1024
PROJ_TM = 1280
PROJ_TN = 512
NORM_ROWS = 256
OUT_TM = 256
KV_CHUNK = 1280
ATT_ROWS = 1024


def _silu(x):
    return x / (1.0 + jnp.exp(-x))


def _cparams(sem):
    return pltpu.CompilerParams(dimension_semantics=sem, vmem_limit_bytes=VMEM_LIMIT_BYTES)


def _mod_kernel(s_ref, w_ref, b_ref, o_ref):
    s = _silu(s_ref[...])
    acc = jnp.dot(s.astype(BF16), w_ref[0].astype(BF16), preferred_element_type=F32)
    o_ref[0] = acc + b_ref[0]


def _modulation(s_in, w_mod, b_mod):
    depth, d, n = w_mod.shape
    tn = 768
    return pl.pallas_call(
        _mod_kernel,
        grid=(depth, n // tn),
        in_specs=[
            pl.BlockSpec((8, d), lambda l, j: (0, 0)),
            pl.BlockSpec((1, d, tn), lambda l, j: (l, 0, j)),
            pl.BlockSpec((1, 1, tn), lambda l, j: (l, 0, j)),
        ],
        out_specs=pl.BlockSpec((1, 8, tn), lambda l, j: (l, 0, j)),
        out_shape=jax.ShapeDtypeStruct((depth, 8, n), F32),
        compiler_params=_cparams(("parallel", "parallel")),
        name="modulation",
    )(s_in, w_mod, b_mod.reshape(depth, 1, n))


def _rope(x, cos, sin, quarter):
    lane = lax.broadcasted_iota(jnp.int32, x.shape, 1)
    first = (lane & quarter) == 0
    partner = jnp.where(first, pltpu.roll(x, LANES - quarter, 1), pltpu.roll(x, quarter, 1))
    return x * cos + partner * sin


def _proj_kernel(x_ref, mod_ref, g_ref, w_ref, cos_ref, sin_ref, qk_ref, o_ref, h_ref, *,
                 n_lat, tm, tn, even):
    i = pl.program_id(0)
    j = pl.program_id(1)
    d = x_ref.shape[1]

    @pl.when(j == 0)
    def _():
        def norm_rows(r, carry):
            r0 = pl.multiple_of(r * NORM_ROWS, NORM_ROWS)
            x = x_ref[pl.ds(r0, NORM_ROWS), :]
            ms = jnp.mean(x * x, axis=-1, keepdims=True)
            y = x * lax.rsqrt(ms + EPS) * g_ref[...]
            row = i * tm + r0 + lax.broadcasted_iota(jnp.int32, (NORM_ROWS, 1), 0)
            is_ctx = row >= n_lat
            shift = jnp.where(is_ctx, mod_ref[1:2, 0:d], mod_ref[0:1, 0:d])
            scale = jnp.where(is_ctx, mod_ref[1:2, d:2 * d], mod_ref[0:1, d:2 * d])
            h_ref[pl.ds(r0, NORM_ROWS), :] = (y * (1.0 + scale) + shift).astype(BF16)
            return carry

        lax.fori_loop(0, tm // NORM_ROWS, norm_rows, 0)

    acc = jnp.dot(h_ref[...], w_ref[...], preferred_element_type=F32)
    groups = tn // LANES

    if even:
        n_rope = 2 * (2 * A_HEADS * A_HEAD_DIM) // tn
        n_q = n_rope // 2
        bk_tile = (4 * A_WIDTH + B_HEADS * B_K_DIM) // tn

        @pl.when(j < n_rope)
        def _():
            cos = cos_ref[...]
            sin = sin_ref[...]
            qscale = jnp.where(j < n_q, A_HEAD_DIM ** -0.5 * LOG2E, 1.0).astype(F32)
            for g in range(groups):
                sl = slice(g * LANES, (g + 1) * LANES)
                o_ref[:, sl] = (_rope(acc[:, sl], cos, sin, A_HEAD_DIM // 4) * qscale).astype(BF16)

        @pl.when(j == bk_tile)
        def _():
            o_ref[...] = (acc * (B_K_DIM ** -0.5)).astype(BF16)

        @pl.when(jnp.logical_and(j >= n_rope, j != bk_tile))
        def _():
            o_ref[...] = acc.astype(BF16)
    else:
        n_q = (C_HEADS * C_HEAD_DIM) // tn
        n_qk = n_q + (C_KV_HEADS * C_HEAD_DIM) // tn

        @pl.when(j < n_qk)
        def _():
            cos = cos_ref[...]
            sin = sin_ref[...]
            is_q = j < n_q
            gain = jnp.where(is_q, qk_ref[0:1, :], qk_ref[1:2, :])
            qscale = jnp.where(is_q, C_HEAD_DIM ** -0.5 * LOG2E, 1.0).astype(F32)
            for g in range(groups):
                sl = slice(g * LANES, (g + 1) * LANES)
                xg = acc[:, sl]
                ms = jnp.mean(xg * xg, axis=-1, keepdims=True)
                yg = xg * lax.rsqrt(ms + EPS) * gain
                o_ref[:, sl] = (_rope(yg, cos, sin, C_HEAD_DIM // 4) * qscale).astype(BF16)

        @pl.when(j >= n_qk)
        def _():
            o_ref[...] = acc.astype(BF16)


def _proj(xs, mod_l, norm_g, w_in, cos, sin, qk_g, *, n_lat, even):
    t, d = xs.shape
    n = w_in.shape[1]
    tm, tn = PROJ_TM, PROJ_TN
    kern = functools.partial(_proj_kernel, n_lat=n_lat, tm=tm, tn=tn, even=even)
    return pl.pallas_call(
        kern,
        grid=(t // tm, n // tn),
        in_specs=[
            pl.BlockSpec((tm, d), lambda i, j: (i, 0)),
            pl.BlockSpec(mod_l.shape, lambda i, j: (0, 0)),
            pl.BlockSpec((1, d), lambda i, j: (0, 0)),
            pl.BlockSpec((d, tn), lambda i, j: (0, j)),
            pl.BlockSpec((tm, LANES), lambda i, j: (i, 0)),
            pl.BlockSpec((tm, LANES), lambda i, j: (i, 0)),
            pl.BlockSpec(qk_g.shape, lambda i, j: (0, 0)),
        ],
        out_specs=pl.BlockSpec((tm, tn), lambda i, j: (i, j)),
        out_shape=jax.ShapeDtypeStruct((t, n), BF16),
        scratch_shapes=[pltpu.VMEM((tm, d), BF16)],
        compiler_params=_cparams(("parallel", "arbitrary")),
        name="proj_even" if even else "proj_odd",
    )(xs, mod_l, norm_g.reshape(1, d), w_in, cos, sin, qk_g)


def _flash_t(qs, k_ref, vt_ref, s_scr, mx_scr, m_scr, acc_scr, ck, n_chunks):
    sub = SUBLANES
    m_rows = qs.shape[0]
    n_col = m_rows // MXU_TILE
    n_kt = ck // MXU_TILE

    def score_block(c, slot, nt, kb, parts):
        cols = slice(nt * MXU_TILE, (nt + 1) * MXU_TILE)
        rows = pl.ds(pl.multiple_of(c * ck + kb * SCORE_ROWS, SCORE_ROWS), SCORE_ROWS)
        s = lax.dot_general(k_ref[rows, :], qs[cols], (((1,), (1,)), ((), ())),
                            preferred_element_type=F32)
        s_scr[slot][kb * SCORE_ROWS:(kb + 1) * SCORE_ROWS, cols] = s
        blk = [s[t * sub:(t + 1) * sub] for t in range(SCORE_ROWS // sub)]
        if parts is None:
            parts, blk = blk[:MAX_CHAINS], blk[MAX_CHAINS:]
        for r, b in enumerate(blk):
            parts[r % MAX_CHAINS] = jnp.maximum(parts[r % MAX_CHAINS], b)
        return parts

    def scores(c, slot, nt):
        cols = slice(nt * MXU_TILE, (nt + 1) * MXU_TILE)
        parts = None
        for kb in range(ck // SCORE_ROWS):
            parts = score_block(c, slot, nt, kb, parts)
        mx_scr[slot][:, cols] = functools.reduce(jnp.maximum, parts)

    def softmax_pv(c, slot, nt):
        cols = slice(nt * MXU_TILE, (nt + 1) * MXU_TILE)
        m_prev = m_scr[:, cols]
        m_new = jnp.maximum(m_prev, jnp.max(mx_scr[slot][:, cols], axis=0, keepdims=True))
        alpha = jnp.exp2(m_prev - m_new)
        m_scr[:, cols] = m_new
        m_blk = jnp.broadcast_to(m_new, (SOFTMAX_ROWS, MXU_TILE))
        acc = alpha * acc_scr[:, cols]
        blocks = []
        for r in range(ck // SOFTMAX_ROWS):
            r0 = r * SOFTMAX_ROWS
            pb = jnp.exp2(s_scr[slot][r0:r0 + SOFTMAX_ROWS, cols] - m_blk)
            blocks.append(pb.astype(BF16))
        p_col = jnp.concatenate(blocks, axis=0)
        acc_scr[:, cols] = acc + jnp.dot(vt_ref[c], p_col, preferred_element_type=F32)

    def step(c, slot, with_scores):
        for nt in range(n_col):
            softmax_pv(c, slot, nt)
            if with_scores:
                scores(c + 1, 1 - slot, nt)

    m_scr[...] = jnp.full(m_scr.shape, -jnp.inf, F32)
    acc_scr[...] = jnp.zeros(acc_scr.shape, F32)
    for nt in range(n_col):
        scores(0, 0, nt)

    def one(c, carry):
        lax.cond(c % 2 == 0, lambda: step(c, 0, True), lambda: step(c, 1, True))
        return carry

    n_pairs = (n_chunks - 1) // 2
    lax.fori_loop(0, 2 * n_pairs, one, 0)
    for c in range(2 * n_pairs, n_chunks):
        step(c, c % 2, c + 1 < n_chunks)
    dv = acc_scr.shape[0] - ONES_ROWS
    return acc_scr[0:dv, :], acc_scr[dv:dv + 1, :]


def _flash_scratch(m_rows, ck, dv):
    two = lambda shape, dtype: [pltpu.VMEM(shape, dtype), pltpu.VMEM(shape, dtype)]
    return [two((ck, m_rows), F32), two((SUBLANES, m_rows), F32), pltpu.VMEM((1, m_rows), F32),
            pltpu.VMEM((dv + ONES_ROWS, m_rows), F32)]


def _v_transposed(p, col0, heads, dv, row0, n_k, ck):
    v = p[row0:row0 + n_k, col0:col0 + heads * dv]
    vt = v.reshape(n_k // ck, ck, heads, dv).transpose(2, 0, 3, 1)
    extra = jnp.zeros(vt.shape[:2] + (ONES_ROWS, ck), vt.dtype).at[:, :, 0, :].set(1)
    return jnp.concatenate([vt, extra], axis=2)


def _gqa_kernel(q_ref, k_ref, vt_ref, g_ref, o_ref, *scratch, tq, ck, n_chunks):
    hd = C_HEAD_DIM
    qs = jnp.concatenate([q_ref[:, h * hd:(h + 1) * hd] for h in range(C_GROUP)], axis=0)
    acc, l = _flash_t(qs, k_ref, vt_ref, *scratch, ck, n_chunks)
    o_t = acc / l
    for h in range(C_GROUP):
        gate = g_ref[:, h * hd:(h + 1) * hd].astype(F32)
        o = o_t[:, h * tq:(h + 1) * tq].T
        o_ref[:, h * hd:(h + 1) * hd] = (o * _silu(gate)).astype(BF16)


def _skip_first_input(kern, *refs):
    return kern(*refs[1:])


def _gqa_attention(p, y_prev, *, q_row0, n_q, k_row0, n_k):
    t = p.shape[0]
    hd = C_HEAD_DIM
    tq = min(ATT_ROWS // C_GROUP, n_q)
    ck = min(KV_CHUNK, n_k)
    n_chunks = n_k // ck
    qb0, kb0 = q_row0 // tq, k_row0 // n_k
    w = C_GROUP * hd
    k_col0 = C_HEADS
    v_col0 = (C_HEADS + C_KV_HEADS) * hd
    g_col0 = (C_HEADS + 2 * C_KV_HEADS) * hd // w
    vt = _v_transposed(p, v_col0, C_KV_HEADS, hd, k_row0, n_k, ck)
    kern = functools.partial(_gqa_kernel, tq=tq, ck=ck, n_chunks=n_chunks)
    in_specs = [
        pl.BlockSpec((tq, w), lambda h, i: (qb0 + i, h)),
        pl.BlockSpec((n_k, hd), lambda h, i: (kb0, k_col0 + h)),
        pl.BlockSpec((None, n_chunks, hd + ONES_ROWS, ck), lambda h, i: (h, 0, 0, 0)),
        pl.BlockSpec((tq, w), lambda h, i: (qb0 + i, g_col0 + h)),
    ]
    args = [p, p, vt, p]
    aliases = {}
    if y_prev is not None:
        in_specs.insert(0, pl.BlockSpec(memory_space=pl.ANY))
        args.insert(0, y_prev)
        aliases = {0: 0}
        kern = functools.partial(_skip_first_input, kern)
    return pl.pallas_call(
        kern,
        grid=(C_KV_HEADS, n_q // tq),
        in_specs=in_specs,
        out_specs=pl.BlockSpec((tq, w), lambda h, i: (qb0 + i, h)),
        out_shape=jax.ShapeDtypeStruct((t, D_MODEL), BF16),
        scratch_shapes=_flash_scratch(C_GROUP * tq, ck, hd),
        input_output_aliases=aliases,
        compiler_params=_cparams(("parallel", "arbitrary")),
        name="gqa_attention",
    )(*args)


def _diff_kernel(lam_ref, sub_ref, q_ref, k_ref, vt_ref, g_ref, o_ref, *scratch, tq, ck,
                 n_chunks, lam_init):
    q = q_ref[...]
    lane = lax.broadcasted_iota(jnp.int32, q.shape, 1)
    zero = jnp.zeros_like(q)
    qs = jnp.concatenate([jnp.where(lane < A_HEAD_DIM, q, zero),
                          jnp.where(lane >= A_HEAD_DIM, q, zero)], axis=0)
    acc, l = _flash_t(qs, k_ref, vt_ref, *scratch, ck, n_chunks)
    o_t = acc / l
    lp = lam_ref[...]
    lam = (jnp.exp(jnp.sum(lp[0:1] * lp[1:2], axis=-1, keepdims=True))
           - jnp.exp(jnp.sum(lp[2:3] * lp[3:4], axis=-1, keepdims=True)) + lam_init)
    od = (o_t[:, :tq] - lam * o_t[:, tq:]).T
    ms = jnp.mean(od * od, axis=-1, keepdims=True)
    y = od * lax.rsqrt(ms + EPS) * sub_ref[...] * (1.0 - lam_init)
    o_ref[...] = (y * _silu(g_ref[...].astype(F32))).astype(BF16)


def _diff_attention(p, lam_p, subln, y_prev, *, lam_init, q_row0, n_q, k_row0, n_k):
    t = p.shape[0]
    tq = min(ATT_ROWS // 2, n_q)
    ck = min(KV_CHUNK, n_k)
    n_chunks = n_k // ck
    qb0, kb0 = q_row0 // tq, k_row0 // n_k
    w = A_V_DIM
    vt = _v_transposed(p, 2 * A_HEADS * w, A_HEADS, w, k_row0, n_k, ck)
    kern = functools.partial(_diff_kernel, tq=tq, ck=ck, n_chunks=n_chunks, lam_init=lam_init)
    in_specs = [
        pl.BlockSpec(lam_p.shape, lambda h, i: (0, 0)),
        pl.BlockSpec((1, w), lambda h, i: (0, 0)),
        pl.BlockSpec((tq, w), lambda h, i: (qb0 + i, h)),
        pl.BlockSpec((n_k, w), lambda h, i: (kb0, A_HEADS + h)),
        pl.BlockSpec((None, n_chunks, w + ONES_ROWS, ck), lambda h, i: (h, 0, 0, 0)),
        pl.BlockSpec((tq, w), lambda h, i: (qb0 + i, 3 * A_HEADS + h)),
    ]
    args = [lam_p, subln.reshape(1, w), p, p, vt, p]
    aliases = {}
    if y_prev is not None:
        in_specs.insert(0, pl.BlockSpec(memory_space=pl.ANY))
        args.insert(0, y_prev)
        aliases = {0: 0}
        kern = functools.partial(_skip_first_input, kern)
    return pl.pallas_call(
        kern,
        grid=(A_HEADS, n_q // tq),
        in_specs=in_specs,
        out_specs=pl.BlockSpec((tq, w), lambda h, i: (qb0 + i, h)),
        out_shape=jax.ShapeDtypeStruct((t, A_WIDTH), BF16),
        scratch_shapes=_flash_scratch(2 * tq, ck, w),
        input_output_aliases=aliases,
        compiler_params=_cparams(("parallel", "arbitrary")),
        name="diff_attention",
    )(*args)


def _ret_kernel(a_ref, qf_ref, kf_ref, vf_ref, qb_ref, kb_ref, vb_ref, of_ref, ob_ref,
                s_ref, dec_ref, qd_ref, kd_ref, cd_ref):
    n = pl.program_id(0)
    c = RET_CHUNK

    @pl.when(n == 0)
    def _():
        i = lax.broadcasted_iota(jnp.int32, (c, c), 0).astype(F32)
        j = lax.broadcasted_iota(jnp.int32, (c, c), 1).astype(F32)
        for d in range(2):
            for h in range(B_HEADS):
                ch = d * B_HEADS + h
                lg = -jnp.exp(a_ref[ch:ch + 1, :])
                rel = (i - j) if d == 0 else (j - i)
                dec_ref[ch] = jnp.where(rel >= 0, jnp.exp(lg * jnp.maximum(rel, 0.0)), 0.0)
                qd_ref[ch] = jnp.exp(lg * ((i + 1.0) if d == 0 else (c - i)))
                kd_ref[ch] = jnp.exp(lg * ((c - 1.0 - i) if d == 0 else i))
                cd_ref[ch] = jnp.exp(lg * float(c)) + jnp.zeros((8, LANES), F32)
                s_ref[ch] = jnp.zeros(s_ref.shape[1:], F32)

    for d, (q_ref, k_ref, v_ref, o_ref) in enumerate(((qf_ref, kf_ref, vf_ref, of_ref),
                                                      (qb_ref, kb_ref, vb_ref, ob_ref))):
        for h in range(B_HEADS):
            ch = d * B_HEADS + h
            q = q_ref[:, h * B_K_DIM:(h + 1) * B_K_DIM]
            k = k_ref[:, h * B_K_DIM:(h + 1) * B_K_DIM]
            v = v_ref[:, h * B_V_DIM:(h + 1) * B_V_DIM]
            st = s_ref[ch]
            sc = lax.dot_general(q, k, (((1,), (1,)), ((), ())),
                                 preferred_element_type=F32) * dec_ref[ch]
            intra = jnp.dot(sc.astype(BF16), v, preferred_element_type=F32)
            qdec = (q.astype(F32) * qd_ref[ch]).astype(BF16)
            cross = jnp.dot(qdec, st.astype(BF16), preferred_element_type=F32)
            o_ref[:, h * B_V_DIM:(h + 1) * B_V_DIM] = intra + cross
            kdec = (k.astype(F32) * kd_ref[ch]).T.astype(BF16)
            upd = jnp.dot(kdec, v, preferred_element_type=F32)
            s_ref[ch] = st * cd_ref[ch][0:1, 0:1] + upd


def _retention(p, ret_decay, *, n_lat):
    t = p.shape[0]
    c = RET_CHUNK
    nc = t // c
    n_lat_c = n_lat // c
    qw, vw = B_HEADS * B_K_DIM, B_HEADS * B_V_DIM
    q_col = 4 * A_WIDTH // qw
    k_col = q_col + 1
    v_col = (4 * A_WIDTH + 2 * qw) // vw
    fwd = lambda n: (n + n_lat_c) % nc
    bwd = lambda n: nc - 1 - n
    return pl.pallas_call(
        _ret_kernel,
        grid=(nc,),
        in_specs=[
            pl.BlockSpec((2 * B_HEADS, 1), lambda n: (0, 0)),
            pl.BlockSpec((c, qw), lambda n: (fwd(n), q_col)),
            pl.BlockSpec((c, qw), lambda n: (fwd(n), k_col)),
            pl.BlockSpec((c, vw), lambda n: (fwd(n), v_col)),
            pl.BlockSpec((c, qw), lambda n: (bwd(n), q_col)),
            pl.BlockSpec((c, qw), lambda n: (bwd(n), k_col)),
            pl.BlockSpec((c, vw), lambda n: (bwd(n), v_col)),
        ],
        out_specs=[
            pl.BlockSpec((c, vw), lambda n: (fwd(n), 0)),
            pl.BlockSpec((c, vw), lambda n: (bwd(n), 0)),
        ],
        out_shape=[jax.ShapeDtypeStruct((t, vw), F32), jax.ShapeDtypeStruct((t, vw), F32)],
        scratch_shapes=[
            pltpu.VMEM((2 * B_HEADS, B_K_DIM, B_V_DIM), F32),
            pltpu.VMEM((2 * B_HEADS, c, c), F32),
            pltpu.VMEM((2 * B_HEADS, c, B_K_DIM), F32),
            pltpu.VMEM((2 * B_HEADS, c, B_K_DIM), F32),
            pltpu.VMEM((2 * B_HEADS, 8, LANES), F32),
        ],
        compiler_params=_cparams(("arbitrary",)),
        name="retention",
    )(ret_decay.reshape(2 * B_HEADS, 1), p, p, p, p, p, p)


def _out_tail(x_ref, y, mod_ref, fin_ref, o_ref, *, n_lat, tm, final):
    d = x_ref.shape[1]
    row = pl.program_id(0) * tm + lax.broadcasted_iota(jnp.int32, (tm, 1), 0)
    gate = jnp.where(row >= n_lat, mod_ref[1:2, 2 * d:3 * d], mod_ref[0:1, 2 * d:3 * d])
    xn = x_ref[...] + gate * y
    if final:
        ms = jnp.mean(xn * xn, axis=-1, keepdims=True)
        xn = xn * lax.rsqrt(ms + EPS) * fin_ref[...]
    o_ref[...] = xn


def _out_even_kernel(x_ref, ya_ref, of_ref, ob_ref, bg_ref, gn_ref, w_ref, mod_ref, fin_ref,
                     o_ref, *, n_lat, tm, final):
    ob = of_ref[...] + ob_ref[...]
    gate = _silu(bg_ref[...].astype(F32))
    parts = []
    for h in range(B_HEADS):
        sl = slice(h * B_V_DIM, (h + 1) * B_V_DIM)
        z = ob[:, sl]
        zc = z - jnp.mean(z, axis=-1, keepdims=True)
        yh = zc * lax.rsqrt(jnp.mean(zc * zc, axis=-1, keepdims=True) + EPS) * gn_ref[:, sl]
        parts.append((yh * gate[:, sl]).astype(BF16))
    yb = jnp.concatenate(parts, axis=1)
    ka = ya_ref.shape[1]
    y = (jnp.dot(ya_ref[...], w_ref[0:ka, :], preferred_element_type=F32)
         + jnp.dot(yb, w_ref[ka:, :], preferred_element_type=F32))
    _out_tail(x_ref, y, mod_ref, fin_ref, o_ref, n_lat=n_lat, tm=tm, final=final)


def _out_odd_kernel(x_ref, y_ref, w_ref, mod_ref, fin_ref, o_ref, *, n_lat, tm, final):
    y = jnp.dot(y_ref[...], w_ref[...], preferred_element_type=F32)
    _out_tail(x_ref, y, mod_ref, fin_ref, o_ref, n_lat=n_lat, tm=tm, final=final)


def _out_proj(xs, mixer_args, mixer_specs, w_out, mod_l, final_norm, *, n_lat, n_rows, even,
              final):
    t, d = xs.shape
    tm = OUT_TM
    body = _out_even_kernel if even else _out_odd_kernel
    kern = functools.partial(body, n_lat=n_lat, tm=tm, final=final)
    row_spec = pl.BlockSpec((tm, d), lambda i: (i, 0))
    const = lambda shape: pl.BlockSpec(shape, lambda i: (0, 0))
    return pl.pallas_call(
        kern,
        grid=(n_rows // tm,),
        in_specs=[row_spec] + mixer_specs + [const(w_out.shape), const(mod_l.shape), const((1, d))],
        out_specs=row_spec,
        out_shape=jax.ShapeDtypeStruct((n_rows, d), F32),
        compiler_params=_cparams(("parallel",)),
        name="out_proj",
    )(xs, *mixer_args, w_out, mod_l, final_norm.reshape(1, d))


def _rope_tables(n_lat, head_dim):
    pos = jnp.arange(n_lat, dtype=jnp.int32)
    rows = (pos // GRID_W).astype(F32)[:, None]
    cols = (pos % GRID_W).astype(F32)[:, None]
    axis_dim = head_dim // 2
    inv = ROPE_THETA ** (-jnp.arange(0, axis_dim, 2, dtype=F32) / axis_dim)
    ar, ac = rows * inv, cols * inv
    cos = jnp.concatenate([jnp.cos(ar), jnp.cos(ar), jnp.cos(ac), jnp.cos(ac)], axis=-1)
    sin = jnp.concatenate([-jnp.sin(ar), jnp.sin(ar), -jnp.sin(ac), jnp.sin(ac)], axis=-1)
    reps = LANES // head_dim
    cos, sin = jnp.tile(cos, (1, reps)), jnp.tile(sin, (1, reps))
    cos = jnp.concatenate([cos, jnp.ones((CTX_LEN, LANES), F32)], axis=0)
    sin = jnp.concatenate([sin, jnp.zeros((CTX_LEN, LANES), F32)], axis=0)
    return cos, sin


def kernel(x, c, ctx, c_ctx, norm_g, w_mod, b_mod, ev_w_in, ev_w_out, diff_lambda, diff_subln,
           ret_decay, ret_gn, od_w_in, od_w_out, qk_norm, final_norm):
    assert x.shape[0] == 1 and ctx.shape[1] == CTX_LEN and x.shape[2] == D_MODEL
    n_lat = x.shape[1]
    t = n_lat + CTX_LEN
    d = D_MODEL
    assert t % PROJ_TM == 0 and t % KV_CHUNK == 0 and n_lat % (ATT_ROWS // 2) == 0

    xs = jnp.concatenate([x[0], ctx[0]], axis=0)
    s_in = jnp.zeros((8, d), F32).at[0].set(c[0]).at[1].set(c_ctx)
    mod = _modulation(s_in, w_mod, b_mod)
    rope_a = _rope_tables(n_lat, A_HEAD_DIM)
    rope_c = _rope_tables(n_lat, C_HEAD_DIM)
    ev_w_in_b, ev_w_out_b = ev_w_in.astype(BF16), ev_w_out.astype(BF16)
    od_w_in_b, od_w_out_b = od_w_in.astype(BF16), od_w_out.astype(BF16)

    for i in range(DEPTH):
        final = i == DEPTH - 1
        n_rows = n_lat if final else t
        tm = OUT_TM
        if i % 2 == 0:
            e = i // 2
            lam_init = 0.8 - 0.6 * math.exp(-0.3 * i)
            p = _proj(xs, mod[i], norm_g[i], ev_w_in_b[e], *rope_a, qk_norm[0], n_lat=n_lat,
                      even=True)
            ya = _diff_attention(p, diff_lambda[e], diff_subln[e], None, lam_init=lam_init,
                                 q_row0=0, n_q=n_lat, k_row0=0, n_k=t)
            if not final:
                ya = _diff_attention(p, diff_lambda[e], diff_subln[e], ya, lam_init=lam_init,
                                     q_row0=n_lat, n_q=CTX_LEN, k_row0=n_lat, n_k=CTX_LEN)
            o_f, o_b = _retention(p, ret_decay[e], n_lat=n_lat)
            bw = B_WIDTH
            args = [ya, o_f, o_b, p, ret_gn[e].reshape(1, bw)]
            specs = [pl.BlockSpec((tm, A_WIDTH), lambda r: (r, 0)),
                     pl.BlockSpec((tm, bw), lambda r: (r, 0)),
                     pl.BlockSpec((tm, bw), lambda r: (r, 0)),
                     pl.BlockSpec((tm, bw), lambda r: (r, (EV_IN - bw) // bw)),
                     pl.BlockSpec((1, bw), lambda r: (0, 0))]
            xs = _out_proj(xs, args, specs, ev_w_out_b[e], mod[i], final_norm, n_lat=n_lat,
                           n_rows=n_rows, even=True, final=final)
        else:
            o = i // 2
            p = _proj(xs, mod[i], norm_g[i], od_w_in_b[o], *rope_c, qk_norm[o], n_lat=n_lat,
                      even=False)
            y = _gqa_attention(p, None, q_row0=0, n_q=n_lat, k_row0=0, n_k=t)
            if not final:
                y = _gqa_attention(p, y, q_row0=n_lat, n_q=CTX_LEN, k_row0=n_lat, n_k=CTX_LEN)
            specs = [pl.BlockSpec((tm, d), lambda r: (r, 0))]
            xs = _out_proj(xs, [y], specs, od_w_out_b[o], mod[i], final_norm, n_lat=n_lat,
                           n_rows=n_rows, even=False, final=final)
    return xs[None]
```

```python
import functools
import math

import jax
import jax.numpy as jnp
from jax import lax
from jax.experimental import pallas as pl
from jax.experimental.pallas import tpu as pltpu

F32 = jnp.float32
BF16 = jnp.bfloat16

D_MODEL = 2048
DEPTH = 4
GRID_W = 64
CTX_LEN = 256
RET_CHUNK = 128
ROPE_THETA = 10000.0
EPS = 1e-6

A_WIDTH = D_MODEL // 2
A_HEADS = 8
A_HEAD_DIM = A_WIDTH // A_HEADS // 2
A_V_DIM = 2 * A_HEAD_DIM
B_WIDTH = D_MODEL - A_WIDTH
B_HEADS = 4
B_V_DIM = B_WIDTH // B_HEADS
B_K_DIM = B_V_DIM // 2
EV_IN = 7168

C_HEADS = 16
C_KV_HEADS = 4
C_GROUP = C_HEADS // C_KV_HEADS
C_HEAD_DIM = D_MODEL // C_HEADS
OD_IN = 5120

LANES = 128
SUBLANES = 8
MXU_TILE = 256
ONES_ROWS = 16
SCORE_ROWS = 256
SOFTMAX_ROWS = 32
MAX_CHAINS = 4
LOG2E = math.log2(math.e)
VMEM_LIMIT_BYTES = 56 * 1024 * 1024
PROJ_TM = 1280
PROJ_TN = 512
NORM_ROWS = 256
OUT_TM = 256
KV_CHUNK = 1280
ATT_ROWS = 1024


def _silu(x):
    return x / (1.0 + jnp.exp(-x))


def _cparams(sem):
    return pltpu.CompilerParams(dimension_semantics=sem, vmem_limit_bytes=VMEM_LIMIT_BYTES)


def _mod_kernel(s_ref, w_ref, b_ref, o_ref):
    s = _silu(s_ref[...])
    acc = jnp.dot(s.astype(BF16), w_ref[0].astype(BF16), preferred_element_type=F32)
    o_ref[0] = acc + b_ref[0]


def _modulation(s_in, w_mod, b_mod):
    depth, d, n = w_mod.shape
    tn = 768
    return pl.pallas_call(
        _mod_kernel,
        grid=(depth, n // tn),
        in_specs=[
            pl.BlockSpec((8, d), lambda l, j: (0, 0)),
            pl.BlockSpec((1, d, tn), lambda l, j: (l, 0, j)),
            pl.BlockSpec((1, 1, tn), lambda l, j: (l, 0, j)),
        ],
        out_specs=pl.BlockSpec((1, 8, tn), lambda l, j: (l, 0, j)),
        out_shape=jax.ShapeDtypeStruct((depth, 8, n), F32),
        compiler_params=_cparams(("parallel", "parallel")),
        name="modulation",
    )(s_in, w_mod, b_mod.reshape(depth, 1, n))


def _rope(x, cos, sin, quarter):
    lane = lax.broadcasted_iota(jnp.int32, x.shape, 1)
    first = (lane & quarter) == 0
    partner = jnp.where(first, pltpu.roll(x, LANES - quarter, 1), pltpu.roll(x, quarter, 1))
    return x * cos + partner * sin


def _proj_kernel(x_ref, mod_ref, g_ref, w_ref, cos_ref, sin_ref, qk_ref, o_ref, h_ref, *,
                 n_lat, tm, tn, even):
    i = pl.program_id(0)
    j = pl.program_id(1)
    d = x_ref.shape[1]

    @pl.when(j == 0)
    def _():
        def norm_rows(r, carry):
            r0 = pl.multiple_of(r * NORM_ROWS, NORM_ROWS)
            x = x_ref[pl.ds(r0, NORM_ROWS), :]
            ms = jnp.mean(x * x, axis=-1, keepdims=True)
            y = x * lax.rsqrt(ms + EPS) * g_ref[...]
            row = i * tm + r0 + lax.broadcasted_iota(jnp.int32, (NORM_ROWS, 1), 0)
            is_ctx = row >= n_lat
            shift = jnp.where(is_ctx, mod_ref[1:2, 0:d], mod_ref[0:1, 0:d])
            scale = jnp.where(is_ctx, mod_ref[1:2, d:2 * d], mod_ref[0:1, d:2 * d])
            h_ref[pl.ds(r0, NORM_ROWS), :] = (y * (1.0 + scale) + shift).astype(BF16)
            return carry

        lax.fori_loop(0, tm // NORM_ROWS, norm_rows, 0)

    acc = jnp.dot(h_ref[...], w_ref[...], preferred_element_type=F32)
    groups = tn // LANES

    if even:
        n_rope = 2 * (2 * A_HEADS * A_HEAD_DIM) // tn
        n_q = n_rope // 2
        bk_tile = (4 * A_WIDTH + B_HEADS * B_K_DIM) // tn

        @pl.when(j < n_rope)
        def _():
            cos = cos_ref[...]
            sin = sin_ref[...]
            qscale = jnp.where(j < n_q, A_HEAD_DIM ** -0.5 * LOG2E, 1.0).astype(F32)
            for g in range(groups):
                sl = slice(g * LANES, (g + 1) * LANES)
                o_ref[:, sl] = (_rope(acc[:, sl], cos, sin, A_HEAD_DIM // 4) * qscale).astype(BF16)

        @pl.when(j == bk_tile)
        def _():
            o_ref[...] = (acc * (B_K_DIM ** -0.5)).astype(BF16)

        @pl.when(jnp.logical_and(j >= n_rope, j != bk_tile))
        def _():
            o_ref[...] = acc.astype(BF16)
    else:
        n_q = (C_HEADS * C_HEAD_DIM) // tn
        n_qk = n_q + (C_KV_HEADS * C_HEAD_DIM) // tn

        @pl.when(j < n_qk)
        def _():
            cos = cos_ref[...]
            sin = sin_ref[...]
            is_q = j < n_q
            gain = jnp.where(is_q, qk_ref[0:1, :], qk_ref[1:2, :])
            qscale = jnp.where(is_q, C_HEAD_DIM ** -0.5 * LOG2E, 1.0).astype(F32)
            for g in range(groups):
                sl = slice(g * LANES, (g + 1) * LANES)
                xg = acc[:, sl]
                ms = jnp.mean(xg * xg, axis=-1, keepdims=True)
                yg = xg * lax.rsqrt(ms + EPS) * gain
                o_ref[:, sl] = (_rope(yg, cos, sin, C_HEAD_DIM // 4) * qscale).astype(BF16)

        @pl.when(j >= n_qk)
        def _():
            o_ref[...] = acc.astype(BF16)


def _proj(xs, mod_l, norm_g, w_in, cos, sin, qk_g, *, n_lat, even):
    t, d = xs.shape
    n = w_in.shape[1]
    tm, tn = PROJ_TM, PROJ_TN
    kern = functools.partial(_proj_kernel, n_lat=n_lat, tm=tm, tn=tn, even=even)
    return pl.pallas_call(
        kern,
        grid=(t // tm, n // tn),
        in_specs=[
            pl.BlockSpec((tm, d), lambda i, j: (i, 0)),
            pl.BlockSpec(mod_l.shape, lambda i, j: (0, 0)),
            pl.BlockSpec((1, d), lambda i, j: (0, 0)),
            pl.BlockSpec((d, tn), lambda i, j: (0, j)),
            pl.BlockSpec((tm, LANES), lambda i, j: (i, 0)),
            pl.BlockSpec((tm, LANES), lambda i, j: (i, 0)),
            pl.BlockSpec(qk_g.shape, lambda i, j: (0, 0)),
        ],
        out_specs=pl.BlockSpec((tm, tn), lambda i, j: (i, j)),
        out_shape=jax.ShapeDtypeStruct((t, n), BF16),
        scratch_shapes=[pltpu.VMEM((tm, d), BF16)],
        compiler_params=_cparams(("parallel", "arbitrary")),
        name="proj_even" if even else "proj_odd",
    )(xs, mod_l, norm_g.reshape(1, d), w_in, cos, sin, qk_g)


def _flash_t(qs, qs_next, k_ref, vt_ref, s_scr, mx_scr, m_scr, acc_scr, ck, n_chunks, tile):
    sub = SUBLANES
    m_rows = qs.shape[0]
    n_col = m_rows // MXU_TILE
    n_kt = ck // MXU_TILE

    def score_block(q, c, slot, nt, kb, parts):
        cols = slice(nt * MXU_TILE, (nt + 1) * MXU_TILE)
        rows = pl.ds(pl.multiple_of(c * ck + kb * SCORE_ROWS, SCORE_ROWS), SCORE_ROWS)
        s = lax.dot_general(k_ref[rows, :], q[cols], (((1,), (1,)), ((), ())),
                            preferred_element_type=F32)
        s_scr[slot][kb * SCORE_ROWS:(kb + 1) * SCORE_ROWS, cols] = s
        blk = [s[t * sub:(t + 1) * sub] for t in range(SCORE_ROWS // sub)]
        if parts is None:
            parts, blk = blk[:MAX_CHAINS], blk[MAX_CHAINS:]
        for r, b in enumerate(blk):
            parts[r % MAX_CHAINS] = jnp.maximum(parts[r % MAX_CHAINS], b)
        return parts

    def scores(q, c, slot, nt):
        cols = slice(nt * MXU_TILE, (nt + 1) * MXU_TILE)
        parts = None
        for kb in range(ck // SCORE_ROWS):
            parts = score_block(q, c, slot, nt, kb, parts)
        mx_scr[slot][:, cols] = functools.reduce(jnp.maximum, parts)

    def softmax_pv(c, slot, nt):
        cols = slice(nt * MXU_TILE, (nt + 1) * MXU_TILE)
        m_prev = m_scr[:, cols]
        m_new = jnp.maximum(m_prev, jnp.max(mx_scr[slot][:, cols], axis=0, keepdims=True))
        alpha = jnp.exp2(m_prev - m_new)
        m_scr[:, cols] = m_new
        m_blk = jnp.broadcast_to(m_new, (SOFTMAX_ROWS, MXU_TILE))
        acc = alpha * acc_scr[:, cols]
        blocks = []
        for r in range(ck // SOFTMAX_ROWS):
            r0 = r * SOFTMAX_ROWS
            pb = jnp.exp2(s_scr[slot][r0:r0 + SOFTMAX_ROWS, cols] - m_blk)
            blocks.append(pb.astype(BF16))
        p_col = jnp.concatenate(blocks, axis=0)
        acc_scr[:, cols] = acc + jnp.dot(vt_ref[c], p_col, preferred_element_type=F32)

    def step(c, slot, q_following, c_following):
        for nt in range(n_col):
            softmax_pv(c, slot, nt)
            scores(q_following, c_following, 1 - slot, nt)

    def on_slot_of(c, fn):
        lax.cond((tile * n_chunks + c) % 2 == 0, lambda: fn(0), lambda: fn(1))

    m_scr[...] = jnp.full(m_scr.shape, -jnp.inf, F32)
    acc_scr[...] = jnp.zeros(acc_scr.shape, F32)

    @pl.when(tile == 0)
    def _():
        for nt in range(n_col):
            scores(qs, 0, 0, nt)

    def one(c, carry):
        on_slot_of(c, lambda slot: step(c, slot, qs, c + 1))
        return carry

    lax.fori_loop(0, n_chunks - 1, one, 0)
    on_slot_of(n_chunks - 1, lambda slot: step(n_chunks - 1, slot, qs_next, 0))
    dv = acc_scr.shape[0] - ONES_ROWS
    return acc_scr[0:dv, :], acc_scr[dv:dv + 1, :]


def _flash_scratch(m_rows, ck, dv):
    two = lambda shape, dtype: [pltpu.VMEM(shape, dtype), pltpu.VMEM(shape, dtype)]
    return [two((ck, m_rows), F32), two((SUBLANES, m_rows), F32), pltpu.VMEM((1, m_rows), F32),
            pltpu.VMEM((dv + ONES_ROWS, m_rows), F32)]


def _v_transposed(p, col0, heads, dv, row0, n_k, ck):
    v = p[row0:row0 + n_k, col0:col0 + heads * dv]
    vt = v.reshape(n_k // ck, ck, heads, dv).transpose(2, 0, 3, 1)
    extra = jnp.zeros(vt.shape[:2] + (ONES_ROWS, ck), vt.dtype).at[:, :, 0, :].set(1)
    return jnp.concatenate([vt, extra], axis=2)


def _gqa_kernel(q_ref, qn_ref, k_ref, vt_ref, g_ref, o_ref, *scratch, tq, ck, n_chunks):
    hd = C_HEAD_DIM
    stack = lambda ref: jnp.concatenate([ref[:, h * hd:(h + 1) * hd] for h in range(C_GROUP)],
                                        axis=0)
    acc, l = _flash_t(stack(q_ref), stack(qn_ref), k_ref, vt_ref, *scratch, ck, n_chunks,
                      pl.program_id(1))
    o_t = acc / l
    for h in range(C_GROUP):
        gate = g_ref[:, h * hd:(h + 1) * hd].astype(F32)
        o = o_t[:, h * tq:(h + 1) * tq].T
        o_ref[:, h * hd:(h + 1) * hd] = (o * _silu(gate)).astype(BF16)


def _skip_first_input(kern, *refs):
    return kern(*refs[1:])


def _gqa_attention(p, y_prev, *, q_row0, n_q, k_row0, n_k):
    t = p.shape[0]
    hd = C_HEAD_DIM
    tq = min(ATT_ROWS // C_GROUP, n_q)
    ck = min(KV_CHUNK, n_k)
    n_chunks = n_k // ck
    qb0, kb0 = q_row0 // tq, k_row0 // n_k
    w = C_GROUP * hd
    k_col0 = C_HEADS
    v_col0 = (C_HEADS + C_KV_HEADS) * hd
    g_col0 = (C_HEADS + 2 * C_KV_HEADS) * hd // w
    vt = _v_transposed(p, v_col0, C_KV_HEADS, hd, k_row0, n_k, ck)
    kern = functools.partial(_gqa_kernel, tq=tq, ck=ck, n_chunks=n_chunks)
    last = n_q // tq - 1
    in_specs = [
        pl.BlockSpec((tq, w), lambda h, i: (qb0 + i, h)),
        pl.BlockSpec((tq, w), lambda h, i: (qb0 + jnp.minimum(i + 1, last), h)),
        pl.BlockSpec((n_k, hd), lambda h, i: (kb0, k_col0 + h)),
        pl.BlockSpec((None, n_chunks, hd + ONES_ROWS, ck), lambda h, i: (h, 0, 0, 0)),
        pl.BlockSpec((tq, w), lambda h, i: (qb0 + i, g_col0 + h)),
    ]
    args = [p, p, p, vt, p]
    aliases = {}
    if y_prev is not None:
        in_specs.insert(0, pl.BlockSpec(memory_space=pl.ANY))
        args.insert(0, y_prev)
        aliases = {0: 0}
        kern = functools.partial(_skip_first_input, kern)
    return pl.pallas_call(
        kern,
        grid=(C_KV_HEADS, n_q // tq),
        in_specs=in_specs,
        out_specs=pl.BlockSpec((tq, w), lambda h, i: (qb0 + i, h)),
        out_shape=jax.ShapeDtypeStruct((t, D_MODEL), BF16),
        scratch_shapes=_flash_scratch(C_GROUP * tq, ck, hd),
        input_output_aliases=aliases,
        compiler_params=_cparams(("parallel", "arbitrary")),
        name="gqa_attention",
    )(*args)


def _diff_kernel(lam_ref, sub_ref, q_ref, qn_ref, k_ref, vt_ref, g_ref, o_ref, *scratch, tq,
                 ck, n_chunks, lam_init):
    def stack(ref):
        q = ref[...]
        lane = lax.broadcasted_iota(jnp.int32, q.shape, 1)
        zero = jnp.zeros_like(q)
        return jnp.concatenate([jnp.where(lane < A_HEAD_DIM, q, zero),
                                jnp.where(lane >= A_HEAD_DIM, q, zero)], axis=0)

    acc, l = _flash_t(stack(q_ref), stack(qn_ref), k_ref, vt_ref, *scratch, ck, n_chunks,
                      pl.program_id(1))
    o_t = acc / l
    lp = lam_ref[...]
    lam = (jnp.exp(jnp.sum(lp[0:1] * lp[1:2], axis=-1, keepdims=True))
           - jnp.exp(jnp.sum(lp[2:3] * lp[3:4], axis=-1, keepdims=True)) + lam_init)
    od = (o_t[:, :tq] - lam * o_t[:, tq:]).T
    ms = jnp.mean(od * od, axis=-1, keepdims=True)
    y = od * lax.rsqrt(ms + EPS) * sub_ref[...] * (1.0 - lam_init)
    o_ref[...] = (y * _silu(g_ref[...].astype(F32))).astype(BF16)


def _diff_attention(p, lam_p, subln, y_prev, *, lam_init, q_row0, n_q, k_row0, n_k):
    t = p.shape[0]
    tq = min(ATT_ROWS // 2, n_q)
    ck = min(KV_CHUNK, n_k)
    n_chunks = n_k // ck
    qb0, kb0 = q_row0 // tq, k_row0 // n_k
    w = A_V_DIM
    last = n_q // tq - 1
    vt = _v_transposed(p, 2 * A_HEADS * w, A_HEADS, w, k_row0, n_k, ck)
    kern = functools.partial(_diff_kernel, tq=tq, ck=ck, n_chunks=n_chunks, lam_init=lam_init)
    in_specs = [
        pl.BlockSpec(lam_p.shape, lambda h, i: (0, 0)),
        pl.BlockSpec((1, w), lambda h, i: (0, 0)),
        pl.BlockSpec((tq, w), lambda h, i: (qb0 + i, h)),
        pl.BlockSpec((tq, w), lambda h, i: (qb0 + jnp.minimum(i + 1, last), h)),
        pl.BlockSpec((n_k, w), lambda h, i: (kb0, A_HEADS + h)),
        pl.BlockSpec((None, n_chunks, w + ONES_ROWS, ck), lambda h, i: (h, 0, 0, 0)),
        pl.BlockSpec((tq, w), lambda h, i: (qb0 + i, 3 * A_HEADS + h)),
    ]
    args = [lam_p, subln.reshape(1, w), p, p, p, vt, p]
    aliases = {}
    if y_prev is not None:
        in_specs.insert(0, pl.BlockSpec(memory_space=pl.ANY))
        args.insert(0, y_prev)
        aliases = {0: 0}
        kern = functools.partial(_skip_first_input, kern)
    return pl.pallas_call(
        kern,
        grid=(A_HEADS, n_q // tq),
        in_specs=in_specs,
        out_specs=pl.BlockSpec((tq, w), lambda h, i: (qb0 + i, h)),
        out_shape=jax.ShapeDtypeStruct((t, A_WIDTH), BF16),
        scratch_shapes=_flash_scratch(2 * tq, ck, w),
        input_output_aliases=aliases,
        compiler_params=_cparams(("parallel", "arbitrary")),
        name="diff_attention",
    )(*args)


def _ret_kernel(a_ref, qf_ref, kf_ref, vf_ref, qb_ref, kb_ref, vb_ref, of_ref, ob_ref,
                s_ref, dec_ref, qd_ref, kd_ref, cd_ref):
    n = pl.program_id(0)
    c = RET_CHUNK

    @pl.when(n == 0)
    def _():
        i = lax.broadcasted_iota(jnp.int32, (c, c), 0).astype(F32)
        j = lax.broadcasted_iota(jnp.int32, (c, c), 1).astype(F32)
        for d in range(2):
            for h in range(B_HEADS):
                ch = d * B_HEADS + h
                lg = -jnp.exp(a_ref[ch:ch + 1, :])
                rel = (i - j) if d == 0 else (j - i)
                dec_ref[ch] = jnp.where(rel >= 0, jnp.exp(lg * jnp.maximum(rel, 0.0)), 0.0)
                qd_ref[ch] = jnp.exp(lg * ((i + 1.0) if d == 0 else (c - i)))
                kd_ref[ch] = jnp.exp(lg * ((c - 1.0 - i) if d == 0 else i))
                cd_ref[ch] = jnp.exp(lg * float(c)) + jnp.zeros((8, LANES), F32)
                s_ref[ch] = jnp.zeros(s_ref.shape[1:], F32)

    for d, (q_ref, k_ref, v_ref, o_ref) in enumerate(((qf_ref, kf_ref, vf_ref, of_ref),
                                                      (qb_ref, kb_ref, vb_ref, ob_ref))):
        for h in range(B_HEADS):
            ch = d * B_HEADS + h
            q = q_ref[:, h * B_K_DIM:(h + 1) * B_K_DIM]
            k = k_ref[:, h * B_K_DIM:(h + 1) * B_K_DIM]
            v = v_ref[:, h * B_V_DIM:(h + 1) * B_V_DIM]
            st = s_ref[ch]
            sc = lax.dot_general(q, k, (((1,), (1,)), ((), ())),
                                 preferred_element_type=F32) * dec_ref[ch]
            intra = jnp.dot(sc.astype(BF16), v, preferred_element_type=F32)
            qdec = (q.astype(F32) * qd_ref[ch]).astype(BF16)
            cross = jnp.dot(qdec, st.astype(BF16), preferred_element_type=F32)
            o_ref[:, h * B_V_DIM:(h + 1) * B_V_DIM] = intra + cross
            kdec = (k.astype(F32) * kd_ref[ch]).T.astype(BF16)
            upd = jnp.dot(kdec, v, preferred_element_type=F32)
            s_ref[ch] = st * cd_ref[ch][0:1, 0:1] + upd


def _retention(p, ret_decay, *, n_lat):
    t = p.shape[0]
    c = RET_CHUNK
    nc = t // c
    n_lat_c = n_lat // c
    qw, vw = B_HEADS * B_K_DIM, B_HEADS * B_V_DIM
    q_col = 4 * A_WIDTH // qw
    k_col = q_col + 1
    v_col = (4 * A_WIDTH + 2 * qw) // vw
    fwd = lambda n: (n + n_lat_c) % nc
    bwd = lambda n: nc - 1 - n
    return pl.pallas_call(
        _ret_kernel,
        grid=(nc,),
        in_specs=[
            pl.BlockSpec((2 * B_HEADS, 1), lambda n: (0, 0)),
            pl.BlockSpec((c, qw), lambda n: (fwd(n), q_col)),
            pl.BlockSpec((c, qw), lambda n: (fwd(n), k_col)),
            pl.BlockSpec((c, vw), lambda n: (fwd(n), v_col)),
            pl.BlockSpec((c, qw), lambda n: (bwd(n), q_col)),
            pl.BlockSpec((c, qw), lambda n: (bwd(n), k_col)),
            pl.BlockSpec((c, vw), lambda n: (bwd(n), v_col)),
        ],
        out_specs=[
            pl.BlockSpec((c, vw), lambda n: (fwd(n), 0)),
            pl.BlockSpec((c, vw), lambda n: (bwd(n), 0)),
        ],
        out_shape=[jax.ShapeDtypeStruct((t, vw), F32), jax.ShapeDtypeStruct((t, vw), F32)],
        scratch_shapes=[
            pltpu.VMEM((2 * B_HEADS, B_K_DIM, B_V_DIM), F32),
            pltpu.VMEM((2 * B_HEADS, c, c), F32),
            pltpu.VMEM((2 * B_HEADS, c, B_K_DIM), F32),
            pltpu.VMEM((2 * B_HEADS, c, B_K_DIM), F32),
            pltpu.VMEM((2 * B_HEADS, 8, LANES), F32),
        ],
        compiler_params=_cparams(("arbitrary",)),
        name="retention",
    )(ret_decay.reshape(2 * B_HEADS, 1), p, p, p, p, p, p)


def _out_tail(x_ref, y, mod_ref, fin_ref, o_ref, *, n_lat, tm, final):
    d = x_ref.shape[1]
    row = pl.program_id(0) * tm + lax.broadcasted_iota(jnp.int32, (tm, 1), 0)
    gate = jnp.where(row >= n_lat, mod_ref[1:2, 2 * d:3 * d], mod_ref[0:1, 2 * d:3 * d])
    xn = x_ref[...] + gate * y
    if final:
        ms = jnp.mean(xn * xn, axis=-1, keepdims=True)
        xn = xn * lax.rsqrt(ms + EPS) * fin_ref[...]
    o_ref[...] = xn


def _out_even_kernel(x_ref, ya_ref, of_ref, ob_ref, bg_ref, gn_ref, w_ref, mod_ref, fin_ref,
                     o_ref, *, n_lat, tm, final):
    ob = of_ref[...] + ob_ref[...]
    gate = _silu(bg_ref[...].astype(F32))
    parts = []
    for h in range(B_HEADS):
        sl = slice(h * B_V_DIM, (h + 1) * B_V_DIM)
        z = ob[:, sl]
        zc = z - jnp.mean(z, axis=-1, keepdims=True)
        yh = zc * lax.rsqrt(jnp.mean(zc * zc, axis=-1, keepdims=True) + EPS) * gn_ref[:, sl]
        parts.append((yh * gate[:, sl]).astype(BF16))
    yb = jnp.concatenate(parts, axis=1)
    ka = ya_ref.shape[1]
    y = (jnp.dot(ya_ref[...], w_ref[0:ka, :], preferred_element_type=F32)
         + jnp.dot(yb, w_ref[ka:, :], preferred_element_type=F32))
    _out_tail(x_ref, y, mod_ref, fin_ref, o_ref, n_lat=n_lat, tm=tm, final=final)


def _out_odd_kernel(x_ref, y_ref, w_ref, mod_ref, fin_ref, o_ref, *, n_lat, tm, final):
    y = jnp.dot(y_ref[...], w_ref[...], preferred_element_type=F32)
    _out_tail(x_ref, y, mod_ref, fin_ref, o_ref, n_lat=n_lat, tm=tm, final=final)


def _out_proj(xs, mixer_args, mixer_specs, w_out, mod_l, final_norm, *, n_lat, n_rows, even,
              final):
    t, d = xs.shape
    tm = OUT_TM
    body = _out_even_kernel if even else _out_odd_kernel
    kern = functools.partial(body, n_lat=n_lat, tm=tm, final=final)
    row_spec = pl.BlockSpec((tm, d), lambda i: (i, 0))
    const = lambda shape: pl.BlockSpec(shape, lambda i: (0, 0))
    return pl.pallas_call(
        kern,
        grid=(n_rows // tm,),
        in_specs=[row_spec] + mixer_specs + [const(w_out.shape), const(mod_l.shape), const((1, d))],
        out_specs=row_spec,
        out_shape=jax.ShapeDtypeStruct((n_rows, d), F32),
        compiler_params=_cparams(("parallel",)),
        name="out_proj",
    )(xs, *mixer_args, w_out, mod_l, final_norm.reshape(1, d))


def _rope_tables(n_lat, head_dim):
    pos = jnp.arange(n_lat, dtype=jnp.int32)
    rows = (pos // GRID_W).astype(F32)[:, None]
    cols = (pos % GRID_W).astype(F32)[:, None]
    axis_dim = head_dim // 2
    inv = ROPE_THETA ** (-jnp.arange(0, axis_dim, 2, dtype=F32) / axis_dim)
    ar, ac = rows * inv, cols * inv
    cos = jnp.concatenate([jnp.cos(ar), jnp.cos(ar), jnp.cos(ac), jnp.cos(ac)], axis=-1)
    sin = jnp.concatenate([-jnp.sin(ar), jnp.sin(ar), -jnp.sin(ac), jnp.sin(ac)], axis=-1)
    reps = LANES // head_dim
    cos, sin = jnp.tile(cos, (1, reps)), jnp.tile(sin, (1, reps))
    cos = jnp.concatenate([cos, jnp.ones((CTX_LEN, LANES), F32)], axis=0)
    sin = jnp.concatenate([sin, jnp.zeros((CTX_LEN, LANES), F32)], axis=0)
    return cos, sin


def kernel(x, c, ctx, c_ctx, norm_g, w_mod, b_mod, ev_w_in, ev_w_out, diff_lambda, diff_subln,
           ret_decay, ret_gn, od_w_in, od_w_out, qk_norm, final_norm):
    assert x.shape[0] == 1 and ctx.shape[1] == CTX_LEN and x.shape[2] == D_MODEL
    n_lat = x.shape[1]
    t = n_lat + CTX_LEN
    d = D_MODEL
    assert t % PROJ_TM == 0 and t % KV_CHUNK == 0 and n_lat % (ATT_ROWS // 2) == 0

    xs = jnp.concatenate([x[0], ctx[0]], axis=0)
    s_in = jnp.zeros((8, d), F32).at[0].set(c[0]).at[1].set(c_ctx)
    mod = _modulation(s_in, w_mod, b_mod)
    rope_a = _rope_tables(n_lat, A_HEAD_DIM)
    rope_c = _rope_tables(n_lat, C_HEAD_DIM)
    ev_w_in_b, ev_w_out_b = ev_w_in.astype(BF16), ev_w_out.astype(BF16)
    od_w_in_b, od_w_out_b = od_w_in.astype(BF16), od_w_out.astype(BF16)

    for i in range(DEPTH):
        final = i == DEPTH - 1
        n_rows = n_lat if final else t
        tm = OUT_TM
        if i % 2 == 0:
            e = i // 2
            lam_init = 0.8 - 0.6 * math.exp(-0.3 * i)
            p = _proj(xs, mod[i], norm_g[i], ev_w_in_b[e], *rope_a, qk_norm[0], n_lat=n_lat,
                      even=True)
            ya = _diff_attention(p, diff_lambda[e], diff_subln[e], None, lam_init=lam_init,
                                 q_row0=0, n_q=n_lat, k_row0=0, n_k=t)
            if not final:
                ya = _diff_attention(p, diff_lambda[e], diff_subln[e], ya, lam_init=lam_init,
                                     q_row0=n_lat, n_q=CTX_LEN, k_row0=n_lat, n_k=CTX_LEN)
            o_f, o_b = _retention(p, ret_decay[e], n_lat=n_lat)
            bw = B_WIDTH
            args = [ya, o_f, o_b, p, ret_gn[e].reshape(1, bw)]
            specs = [pl.BlockSpec((tm, A_WIDTH), lambda r: (r, 0)),
                     pl.BlockSpec((tm, bw), lambda r: (r, 0)),
                     pl.BlockSpec((tm, bw), lambda r: (r, 0)),
                     pl.BlockSpec((tm, bw), lambda r: (r, (EV_IN - bw) // bw)),
                     pl.BlockSpec((1, bw), lambda r: (0, 0))]
            xs = _out_proj(xs, args, specs, ev_w_out_b[e], mod[i], final_norm, n_lat=n_lat,
                           n_rows=n_rows, even=True, final=final)
        else:
            o = i // 2
            p = _proj(xs, mod[i], norm_g[i], od_w_in_b[o], *rope_c, qk_norm[o], n_lat=n_lat,
                      even=False)
            y = _gqa_attention(p, None, q_row0=0, n_q=n_lat, k_row0=0, n_k=t)
            if not final:
                y = _gqa_attention(p, y, q_row0=n_lat, n_q=CTX_LEN, k_row0=n_lat, n_k=CTX_LEN)
            specs = [pl.BlockSpec((tm, d), lambda r: (r, 0))]
            xs = _out_proj(xs, [y], specs, od_w_out_b[o], mod[i], final_norm, n_lat=n_lat,
                           n_rows=n_rows, even=False, final=final)
    return xs[None]
```

```python
import functools
import math

import jax
import jax.numpy as jnp
from jax import lax
from jax.experimental import pallas as pl
from jax.experimental.pallas import tpu as pltpu

F32 = jnp.float32
BF16 = jnp.bfloat16

D_MODEL = 2048
DEPTH = 4
GRID_W = 64
CTX_LEN = 256
RET_CHUNK = 128
ROPE_THETA = 10000.0
EPS = 1e-6

A_WIDTH = D_MODEL // 2
A_HEADS = 8
A_HEAD_DIM = A_WIDTH // A_HEADS // 2
A_V_DIM = 2 * A_HEAD_DIM
B_WIDTH = D_MODEL - A_WIDTH
B_HEADS = 4
B_V_DIM = B_WIDTH // B_HEADS
B_K_DIM = B_V_DIM // 2
EV_IN = 7168

C_HEADS = 16
C_KV_HEADS = 4
C_GROUP = C_HEADS // C_KV_HEADS
C_HEAD_DIM = D_MODEL // C_HEADS
OD_IN = 5120

LANES = 128
SUBLANES = 8
MXU_TILE = 256
ONES_ROWS = 16
SCORE_ROWS = 256
SOFTMAX_ROWS = 32
MAX_CHAINS = 4
LOG2E = math.log2(math.e)
VMEM_LIMIT_BYTES = 56 * 1024 * 1024
PROJ_TM = 1280
PROJ_TN = 512
NORM_ROWS = 256
OUT_TM = 256
KV_CHUNK = 1280
ATT_ROWS = 2048


def _silu(x):
    return x / (1.0 + jnp.exp(-x))


def _cparams(sem):
    return pltpu.CompilerParams(dimension_semantics=sem, vmem_limit_bytes=VMEM_LIMIT_BYTES)


def _mod_kernel(s_ref, w_ref, b_ref, o_ref):
    s = _silu(s_ref[...])
    acc = jnp.dot(s.astype(BF16), w_ref[0].astype(BF16), preferred_element_type=F32)
    o_ref[0] = acc + b_ref[0]


def _modulation(s_in, w_mod, b_mod):
    depth, d, n = w_mod.shape
    tn = 768
    return pl.pallas_call(
        _mod_kernel,
        grid=(depth, n // tn),
        in_specs=[
            pl.BlockSpec((8, d), lambda l, j: (0, 0)),
            pl.BlockSpec((1, d, tn), lambda l, j: (l, 0, j)),
            pl.BlockSpec((1, 1, tn), lambda l, j: (l, 0, j)),
        ],
        out_specs=pl.BlockSpec((1, 8, tn), lambda l, j: (l, 0, j)),
        out_shape=jax.ShapeDtypeStruct((depth, 8, n), F32),
        compiler_params=_cparams(("parallel", "parallel")),
        name="modulation",
    )(s_in, w_mod, b_mod.reshape(depth, 1, n))


def _rope(x, cos, sin, quarter):
    lane = lax.broadcasted_iota(jnp.int32, x.shape, 1)
    first = (lane & quarter) == 0
    partner = jnp.where(first, pltpu.roll(x, LANES - quarter, 1), pltpu.roll(x, quarter, 1))
    return x * cos + partner * sin


def _proj_kernel(x_ref, mod_ref, g_ref, w_ref, cos_ref, sin_ref, qk_ref, o_ref, h_ref, *,
                 n_lat, tm, tn, even):
    i = pl.program_id(0)
    j = pl.program_id(1)
    d = x_ref.shape[1]

    @pl.when(j == 0)
    def _():
        def norm_rows(r, carry):
            r0 = pl.multiple_of(r * NORM_ROWS, NORM_ROWS)
            x = x_ref[pl.ds(r0, NORM_ROWS), :]
            ms = jnp.mean(x * x, axis=-1, keepdims=True)
            y = x * lax.rsqrt(ms + EPS) * g_ref[...]
            row = i * tm + r0 + lax.broadcasted_iota(jnp.int32, (NORM_ROWS, 1), 0)
            is_ctx = row >= n_lat
            shift = jnp.where(is_ctx, mod_ref[1:2, 0:d], mod_ref[0:1, 0:d])
            scale = jnp.where(is_ctx, mod_ref[1:2, d:2 * d], mod_ref[0:1, d:2 * d])
            h_ref[pl.ds(r0, NORM_ROWS), :] = (y * (1.0 + scale) + shift).astype(BF16)
            return carry

        lax.fori_loop(0, tm // NORM_ROWS, norm_rows, 0)

    acc = jnp.dot(h_ref[...], w_ref[...], preferred_element_type=F32)
    groups = tn // LANES

    if even:
        n_rope = 2 * (2 * A_HEADS * A_HEAD_DIM) // tn
        n_q = n_rope // 2
        bk_tile = (4 * A_WIDTH + B_HEADS * B_K_DIM) // tn

        @pl.when(j < n_rope)
        def _():
            cos = cos_ref[...]
            sin = sin_ref[...]
            qscale = jnp.where(j < n_q, A_HEAD_DIM ** -0.5 * LOG2E, 1.0).astype(F32)
            for g in range(groups):
                sl = slice(g * LANES, (g + 1) * LANES)
                o_ref[:, sl] = (_rope(acc[:, sl], cos, sin, A_HEAD_DIM // 4) * qscale).astype(BF16)

        @pl.when(j == bk_tile)
        def _():
            o_ref[...] = (acc * (B_K_DIM ** -0.5)).astype(BF16)

        @pl.when(jnp.logical_and(j >= n_rope, j != bk_tile))
        def _():
            o_ref[...] = acc.astype(BF16)
    else:
        n_q = (C_HEADS * C_HEAD_DIM) // tn
        n_qk = n_q + (C_KV_HEADS * C_HEAD_DIM) // tn

        @pl.when(j < n_qk)
        def _():
            cos = cos_ref[...]
            sin = sin_ref[...]
            is_q = j < n_q
            gain = jnp.where(is_q, qk_ref[0:1, :], qk_ref[1:2, :])
            qscale = jnp.where(is_q, C_HEAD_DIM ** -0.5 * LOG2E, 1.0).astype(F32)
            for g in range(groups):
                sl = slice(g * LANES, (g + 1) * LANES)
                xg = acc[:, sl]
                ms = jnp.mean(xg * xg, axis=-1, keepdims=True)
                yg = xg * lax.rsqrt(ms + EPS) * gain
                o_ref[:, sl] = (_rope(yg, cos, sin, C_HEAD_DIM // 4) * qscale).astype(BF16)

        @pl.when(j >= n_qk)
        def _():
            o_ref[...] = acc.astype(BF16)


def _proj(xs, mod_l, norm_g, w_in, cos, sin, qk_g, *, n_lat, even):
    t, d = xs.shape
    n = w_in.shape[1]
    tm, tn = PROJ_TM, PROJ_TN
    kern = functools.partial(_proj_kernel, n_lat=n_lat, tm=tm, tn=tn, even=even)
    return pl.pallas_call(
        kern,
        grid=(t // tm, n // tn),
        in_specs=[
            pl.BlockSpec((tm, d), lambda i, j: (i, 0)),
            pl.BlockSpec(mod_l.shape, lambda i, j: (0, 0)),
            pl.BlockSpec((1, d), lambda i, j: (0, 0)),
            pl.BlockSpec((d, tn), lambda i, j: (0, j)),
            pl.BlockSpec((tm, LANES), lambda i, j: (i, 0)),
            pl.BlockSpec((tm, LANES), lambda i, j: (i, 0)),
            pl.BlockSpec(qk_g.shape, lambda i, j: (0, 0)),
        ],
        out_specs=pl.BlockSpec((tm, tn), lambda i, j: (i, j)),
        out_shape=jax.ShapeDtypeStruct((t, n), BF16),
        scratch_shapes=[pltpu.VMEM((tm, d), BF16)],
        compiler_params=_cparams(("parallel", "arbitrary")),
        name="proj_even" if even else "proj_odd",
    )(xs, mod_l, norm_g.reshape(1, d), w_in, cos, sin, qk_g)


def _flash_t(qs, qs_next, k_ref, vt_ref, s_scr, mx_scr, m_scr, acc_scr, ck, n_chunks, tile):
    sub = SUBLANES
    m_rows = qs.shape[0]
    n_col = m_rows // MXU_TILE
    n_kt = ck // MXU_TILE

    def score_block(q, c, slot, nt, kb, parts):
        cols = slice(nt * MXU_TILE, (nt + 1) * MXU_TILE)
        rows = pl.ds(pl.multiple_of(c * ck + kb * SCORE_ROWS, SCORE_ROWS), SCORE_ROWS)
        s = lax.dot_general(k_ref[rows, :], q[cols], (((1,), (1,)), ((), ())),
                            preferred_element_type=F32)
        s_scr[slot][kb * SCORE_ROWS:(kb + 1) * SCORE_ROWS, cols] = s
        blk = [s[t * sub:(t + 1) * sub] for t in range(SCORE_ROWS // sub)]
        if parts is None:
            parts, blk = blk[:MAX_CHAINS], blk[MAX_CHAINS:]
        for r, b in enumerate(blk):
            parts[r % MAX_CHAINS] = jnp.maximum(parts[r % MAX_CHAINS], b)
        return parts

    def scores(q, c, slot, nt):
        cols = slice(nt * MXU_TILE, (nt + 1) * MXU_TILE)
        parts = None
        for kb in range(ck // SCORE_ROWS):
            parts = score_block(q, c, slot, nt, kb, parts)
        mx_scr[slot][:, cols] = functools.reduce(jnp.maximum, parts)

    def softmax_pv(c, slot, nt):
        cols = slice(nt * MXU_TILE, (nt + 1) * MXU_TILE)
        m_prev = m_scr[:, cols]
        m_new = jnp.maximum(m_prev, jnp.max(mx_scr[slot][:, cols], axis=0, keepdims=True))
        alpha = jnp.exp2(m_prev - m_new)
        m_scr[:, cols] = m_new
        m_blk = jnp.broadcast_to(m_new, (SOFTMAX_ROWS, MXU_TILE))
        acc = alpha * acc_scr[:, cols]
        blocks = []
        for r in range(ck // SOFTMAX_ROWS):
            r0 = r * SOFTMAX_ROWS
            pb = jnp.exp2(s_scr[slot][r0:r0 + SOFTMAX_ROWS, cols] - m_blk)
            blocks.append(pb.astype(BF16))
        p_col = jnp.concatenate(blocks, axis=0)
        acc_scr[:, cols] = acc + jnp.dot(vt_ref[c], p_col, preferred_element_type=F32)

    def step(c, slot, q_following, c_following):
        for nt in range(n_col):
            softmax_pv(c, slot, nt)
            scores(q_following, c_following, 1 - slot, nt)

    def on_slot_of(c, fn):
        lax.cond((tile * n_chunks + c) % 2 == 0, lambda: fn(0), lambda: fn(1))

    m_scr[...] = jnp.full(m_scr.shape, -jnp.inf, F32)
    acc_scr[...] = jnp.zeros(acc_scr.shape, F32)

    @pl.when(tile == 0)
    def _():
        for nt in range(n_col):
            scores(qs, 0, 0, nt)

    def one(c, carry):
        on_slot_of(c, lambda slot: step(c, slot, qs, c + 1))
        return carry

    lax.fori_loop(0, n_chunks - 1, one, 0)
    on_slot_of(n_chunks - 1, lambda slot: step(n_chunks - 1, slot, qs_next, 0))
    dv = acc_scr.shape[0] - ONES_ROWS
    return acc_scr[0:dv, :], acc_scr[dv:dv + 1, :]


def _flash_scratch(m_rows, ck, dv):
    two = lambda shape, dtype: [pltpu.VMEM(shape, dtype), pltpu.VMEM(shape, dtype)]
    return [two((ck, m_rows), F32), two((SUBLANES, m_rows), F32), pltpu.VMEM((1, m_rows), F32),
            pltpu.VMEM((dv + ONES_ROWS, m_rows), F32)]


def _v_transposed(p, col0, heads, dv, row0, n_k, ck):
    v = p[row0:row0 + n_k, col0:col0 + heads * dv]
    vt = v.reshape(n_k // ck, ck, heads, dv).transpose(2, 0, 3, 1)
    extra = jnp.zeros(vt.shape[:2] + (ONES_ROWS, ck), vt.dtype).at[:, :, 0, :].set(1)
    return jnp.concatenate([vt, extra], axis=2)


def _gqa_kernel(q_ref, qn_ref, k_ref, vt_ref, g_ref, o_ref, *scratch, tq, ck, n_chunks):
    hd = C_HEAD_DIM
    stack = lambda ref: jnp.concatenate([ref[:, h * hd:(h + 1) * hd] for h in range(C_GROUP)],
                                        axis=0)
    acc, l = _flash_t(stack(q_ref), stack(qn_ref), k_ref, vt_ref, *scratch, ck, n_chunks,
                      pl.program_id(1))
    o_t = acc / l
    for h in range(C_GROUP):
        gate = g_ref[:, h * hd:(h + 1) * hd].astype(F32)
        o = o_t[:, h * tq:(h + 1) * tq].T
        o_ref[:, h * hd:(h + 1) * hd] = (o * _silu(gate)).astype(BF16)


def _skip_first_input(kern, *refs):
    return kern(*refs[1:])


def _gqa_attention(p, y_prev, *, q_row0, n_q, k_row0, n_k):
    t = p.shape[0]
    hd = C_HEAD_DIM
    tq = min(ATT_ROWS // C_GROUP, n_q)
    ck = min(KV_CHUNK, n_k)
    n_chunks = n_k // ck
    qb0, kb0 = q_row0 // tq, k_row0 // n_k
    w = C_GROUP * hd
    k_col0 = C_HEADS
    v_col0 = (C_HEADS + C_KV_HEADS) * hd
    g_col0 = (C_HEADS + 2 * C_KV_HEADS) * hd // w
    vt = _v_transposed(p, v_col0, C_KV_HEADS, hd, k_row0, n_k, ck)
    kern = functools.partial(_gqa_kernel, tq=tq, ck=ck, n_chunks=n_chunks)
    last = n_q // tq - 1
    in_specs = [
        pl.BlockSpec((tq, w), lambda h, i: (qb0 + i, h)),
        pl.BlockSpec((tq, w), lambda h, i: (qb0 + jnp.minimum(i + 1, last), h)),
        pl.BlockSpec((n_k, hd), lambda h, i: (kb0, k_col0 + h)),
        pl.BlockSpec((None, n_chunks, hd + ONES_ROWS, ck), lambda h, i: (h, 0, 0, 0)),
        pl.BlockSpec((tq, w), lambda h, i: (qb0 + i, g_col0 + h)),
    ]
    args = [p, p, p, vt, p]
    aliases = {}
    if y_prev is not None:
        in_specs.insert(0, pl.BlockSpec(memory_space=pl.ANY))
        args.insert(0, y_prev)
        aliases = {0: 0}
        kern = functools.partial(_skip_first_input, kern)
    return pl.pallas_call(
        kern,
        grid=(C_KV_HEADS, n_q // tq),
        in_specs=in_specs,
        out_specs=pl.BlockSpec((tq, w), lambda h, i: (qb0 + i, h)),
        out_shape=jax.ShapeDtypeStruct((t, D_MODEL), BF16),
        scratch_shapes=_flash_scratch(C_GROUP * tq, ck, hd),
        input_output_aliases=aliases,
        compiler_params=_cparams(("parallel", "arbitrary")),
        name="gqa_attention",
    )(*args)


def _diff_kernel(lam_ref, sub_ref, q_ref, qn_ref, k_ref, vt_ref, g_ref, o_ref, *scratch, tq,
                 ck, n_chunks, lam_init):
    def stack(ref):
        q = ref[...]
        lane = lax.broadcasted_iota(jnp.int32, q.shape, 1)
        zero = jnp.zeros_like(q)
        return jnp.concatenate([jnp.where(lane < A_HEAD_DIM, q, zero),
                                jnp.where(lane >= A_HEAD_DIM, q, zero)], axis=0)

    acc, l = _flash_t(stack(q_ref), stack(qn_ref), k_ref, vt_ref, *scratch, ck, n_chunks,
                      pl.program_id(1))
    o_t = acc / l
    lp = lam_ref[...]
    lam = (jnp.exp(jnp.sum(lp[0:1] * lp[1:2], axis=-1, keepdims=True))
           - jnp.exp(jnp.sum(lp[2:3] * lp[3:4], axis=-1, keepdims=True)) + lam_init)
    od = (o_t[:, :tq] - lam * o_t[:, tq:]).T
    ms = jnp.mean(od * od, axis=-1, keepdims=True)
    y = od * lax.rsqrt(ms + EPS) * sub_ref[...] * (1.0 - lam_init)
    o_ref[...] = (y * _silu(g_ref[...].astype(F32))).astype(BF16)


def _diff_attention(p, lam_p, subln, y_prev, *, lam_init, q_row0, n_q, k_row0, n_k):
    t = p.shape[0]
    tq = min(ATT_ROWS // 2, n_q)
    ck = min(KV_CHUNK, n_k)
    n_chunks = n_k // ck
    qb0, kb0 = q_row0 // tq, k_row0 // n_k
    w = A_V_DIM
    last = n_q // tq - 1
    vt = _v_transposed(p, 2 * A_HEADS * w, A_HEADS, w, k_row0, n_k, ck)
    kern = functools.partial(_diff_kernel, tq=tq, ck=ck, n_chunks=n_chunks, lam_init=lam_init)
    in_specs = [
        pl.BlockSpec(lam_p.shape, lambda h, i: (0, 0)),
        pl.BlockSpec((1, w), lambda h, i: (0, 0)),
        pl.BlockSpec((tq, w), lambda h, i: (qb0 + i, h)),
        pl.BlockSpec((tq, w), lambda h, i: (qb0 + jnp.minimum(i + 1, last), h)),
        pl.BlockSpec((n_k, w), lambda h, i: (kb0, A_HEADS + h)),
        pl.BlockSpec((None, n_chunks, w + ONES_ROWS, ck), lambda h, i: (h, 0, 0, 0)),
        pl.BlockSpec((tq, w), lambda h, i: (qb0 + i, 3 * A_HEADS + h)),
    ]
    args = [lam_p, subln.reshape(1, w), p, p, p, vt, p]
    aliases = {}
    if y_prev is not None:
        in_specs.insert(0, pl.BlockSpec(memory_space=pl.ANY))
        args.insert(0, y_prev)
        aliases = {0: 0}
        kern = functools.partial(_skip_first_input, kern)
    return pl.pallas_call(
        kern,
        grid=(A_HEADS, n_q // tq),
        in_specs=in_specs,
        out_specs=pl.BlockSpec((tq, w), lambda h, i: (qb0 + i, h)),
        out_shape=jax.ShapeDtypeStruct((t, A_WIDTH), BF16),
        scratch_shapes=_flash_scratch(2 * tq, ck, w),
        input_output_aliases=aliases,
        compiler_params=_cparams(("parallel", "arbitrary")),
        name="diff_attention",
    )(*args)


def _ret_kernel(a_ref, qf_ref, kf_ref, vf_ref, qb_ref, kb_ref, vb_ref, of_ref, ob_ref,
                s_ref, dec_ref, qd_ref, kd_ref, cd_ref):
    n = pl.program_id(0)
    c = RET_CHUNK

    @pl.when(n == 0)
    def _():
        i = lax.broadcasted_iota(jnp.int32, (c, c), 0).astype(F32)
        j = lax.broadcasted_iota(jnp.int32, (c, c), 1).astype(F32)
        for d in range(2):
            for h in range(B_HEADS):
                ch = d * B_HEADS + h
                lg = -jnp.exp(a_ref[ch:ch + 1, :])
                rel = (i - j) if d == 0 else (j - i)
                dec_ref[ch] = jnp.where(rel >= 0, jnp.exp(lg * jnp.maximum(rel, 0.0)), 0.0)
                qd_ref[ch] = jnp.exp(lg * ((i + 1.0) if d == 0 else (c - i)))
                kd_ref[ch] = jnp.exp(lg * ((c - 1.0 - i) if d == 0 else i))
                cd_ref[ch] = jnp.exp(lg * float(c)) + jnp.zeros((8, LANES), F32)
                s_ref[ch] = jnp.zeros(s_ref.shape[1:], F32)

    for d, (q_ref, k_ref, v_ref, o_ref) in enumerate(((qf_ref, kf_ref, vf_ref, of_ref),
                                                      (qb_ref, kb_ref, vb_ref, ob_ref))):
        for h in range(B_HEADS):
            ch = d * B_HEADS + h
            q = q_ref[:, h * B_K_DIM:(h + 1) * B_K_DIM]
            k = k_ref[:, h * B_K_DIM:(h + 1) * B_K_DIM]
            v = v_ref[:, h * B_V_DIM:(h + 1) * B_V_DIM]
            st = s_ref[ch]
            sc = lax.dot_general(q, k, (((1,), (1,)), ((), ())),
                                 preferred_element_type=F32) * dec_ref[ch]
            intra = jnp.dot(sc.astype(BF16), v, preferred_element_type=F32)
            qdec = (q.astype(F32) * qd_ref[ch]).astype(BF16)
            cross = jnp.dot(qdec, st.astype(BF16), preferred_element_type=F32)
            o_ref[:, h * B_V_DIM:(h + 1) * B_V_DIM] = intra + cross
            kdec = (k.astype(F32) * kd_ref[ch]).T.astype(BF16)
            upd = jnp.dot(kdec, v, preferred_element_type=F32)
            s_ref[ch] = st * cd_ref[ch][0:1, 0:1] + upd


def _retention(p, ret_decay, *, n_lat):
    t = p.shape[0]
    c = RET_CHUNK
    nc = t // c
    n_lat_c = n_lat // c
    qw, vw = B_HEADS * B_K_DIM, B_HEADS * B_V_DIM
    q_col = 4 * A_WIDTH // qw
    k_col = q_col + 1
    v_col = (4 * A_WIDTH + 2 * qw) // vw
    fwd = lambda n: (n + n_lat_c) % nc
    bwd = lambda n: nc - 1 - n
    return pl.pallas_call(
        _ret_kernel,
        grid=(nc,),
        in_specs=[
            pl.BlockSpec((2 * B_HEADS, 1), lambda n: (0, 0)),
            pl.BlockSpec((c, qw), lambda n: (fwd(n), q_col)),
            pl.BlockSpec((c, qw), lambda n: (fwd(n), k_col)),
            pl.BlockSpec((c, vw), lambda n: (fwd(n), v_col)),
            pl.BlockSpec((c, qw), lambda n: (bwd(n), q_col)),
            pl.BlockSpec((c, qw), lambda n: (bwd(n), k_col)),
            pl.BlockSpec((c, vw), lambda n: (bwd(n), v_col)),
        ],
        out_specs=[
            pl.BlockSpec((c, vw), lambda n: (fwd(n), 0)),
            pl.BlockSpec((c, vw), lambda n: (bwd(n), 0)),
        ],
        out_shape=[jax.ShapeDtypeStruct((t, vw), F32), jax.ShapeDtypeStruct((t, vw), F32)],
        scratch_shapes=[
            pltpu.VMEM((2 * B_HEADS, B_K_DIM, B_V_DIM), F32),
            pltpu.VMEM((2 * B_HEADS, c, c), F32),
            pltpu.VMEM((2 * B_HEADS, c, B_K_DIM), F32),
            pltpu.VMEM((2 * B_HEADS, c, B_K_DIM), F32),
            pltpu.VMEM((2 * B_HEADS, 8, LANES), F32),
        ],
        compiler_params=_cparams(("arbitrary",)),
        name="retention",
    )(ret_decay.reshape(2 * B_HEADS, 1), p, p, p, p, p, p)


def _out_tail(x_ref, y, mod_ref, fin_ref, o_ref, *, n_lat, tm, final):
    d = x_ref.shape[1]
    row = pl.program_id(0) * tm + lax.broadcasted_iota(jnp.int32, (tm, 1), 0)
    gate = jnp.where(row >= n_lat, mod_ref[1:2, 2 * d:3 * d], mod_ref[0:1, 2 * d:3 * d])
    xn = x_ref[...] + gate * y
    if final:
        ms = jnp.mean(xn * xn, axis=-1, keepdims=True)
        xn = xn * lax.rsqrt(ms + EPS) * fin_ref[...]
    o_ref[...] = xn


def _out_even_kernel(x_ref, ya_ref, of_ref, ob_ref, bg_ref, gn_ref, w_ref, mod_ref, fin_ref,
                     o_ref, *, n_lat, tm, final):
    ob = of_ref[...] + ob_ref[...]
    gate = _silu(bg_ref[...].astype(F32))
    parts = []
    for h in range(B_HEADS):
        sl = slice(h * B_V_DIM, (h + 1) * B_V_DIM)
        z = ob[:, sl]
        zc = z - jnp.mean(z, axis=-1, keepdims=True)
        yh = zc * lax.rsqrt(jnp.mean(zc * zc, axis=-1, keepdims=True) + EPS) * gn_ref[:, sl]
        parts.append((yh * gate[:, sl]).astype(BF16))
    yb = jnp.concatenate(parts, axis=1)
    ka = ya_ref.shape[1]
    y = (jnp.dot(ya_ref[...], w_ref[0:ka, :], preferred_element_type=F32)
         + jnp.dot(yb, w_ref[ka:, :], preferred_element_type=F32))
    _out_tail(x_ref, y, mod_ref, fin_ref, o_ref, n_lat=n_lat, tm=tm, final=final)


def _out_odd_kernel(x_ref, y_ref, w_ref, mod_ref, fin_ref, o_ref, *, n_lat, tm, final):
    y = jnp.dot(y_ref[...], w_ref[...], preferred_element_type=F32)
    _out_tail(x_ref, y, mod_ref, fin_ref, o_ref, n_lat=n_lat, tm=tm, final=final)


def _out_proj(xs, mixer_args, mixer_specs, w_out, mod_l, final_norm, *, n_lat, n_rows, even,
              final):
    t, d = xs.shape
    tm = OUT_TM
    body = _out_even_kernel if even else _out_odd_kernel
    kern = functools.partial(body, n_lat=n_lat, tm=tm, final=final)
    row_spec = pl.BlockSpec((tm, d), lambda i: (i, 0))
    const = lambda shape: pl.BlockSpec(shape, lambda i: (0, 0))
    return pl.pallas_call(
        kern,
        grid=(n_rows // tm,),
        in_specs=[row_spec] + mixer_specs + [const(w_out.shape), const(mod_l.shape), const((1, d))],
        out_specs=row_spec,
        out_shape=jax.ShapeDtypeStruct((n_rows, d), F32),
        compiler_params=_cparams(("parallel",)),
        name="out_proj",
    )(xs, *mixer_args, w_out, mod_l, final_norm.reshape(1, d))


def _rope_tables(n_lat, head_dim):
    pos = jnp.arange(n_lat, dtype=jnp.int32)
    rows = (pos // GRID_W).astype(F32)[:, None]
    cols = (pos % GRID_W).astype(F32)[:, None]
    axis_dim = head_dim // 2
    inv = ROPE_THETA ** (-jnp.arange(0, axis_dim, 2, dtype=F32) / axis_dim)
    ar, ac = rows * inv, cols * inv
    cos = jnp.concatenate([jnp.cos(ar), jnp.cos(ar), jnp.cos(ac), jnp.cos(ac)], axis=-1)
    sin = jnp.concatenate([-jnp.sin(ar), jnp.sin(ar), -jnp.sin(ac), jnp.sin(ac)], axis=-1)
    reps = LANES // head_dim
    cos, sin = jnp.tile(cos, (1, reps)), jnp.tile(sin, (1, reps))
    cos = jnp.concatenate([cos, jnp.ones((CTX_LEN, LANES), F32)], axis=0)
    sin = jnp.concatenate([sin, jnp.zeros((CTX_LEN, LANES), F32)], axis=0)
    return cos, sin


def kernel(x, c, ctx, c_ctx, norm_g, w_mod, b_mod, ev_w_in, ev_w_out, diff_lambda, diff_subln,
           ret_decay, ret_gn, od_w_in, od_w_out, qk_norm, final_norm):
    assert x.shape[0] == 1 and ctx.shape[1] == CTX_LEN and x.shape[2] == D_MODEL
    n_lat = x.shape[1]
    t = n_lat + CTX_LEN
    d = D_MODEL
    assert t % PROJ_TM == 0 and t % KV_CHUNK == 0 and n_lat % (ATT_ROWS // 2) == 0

    xs = jnp.concatenate([x[0], ctx[0]], axis=0)
    s_in = jnp.zeros((8, d), F32).at[0].set(c[0]).at[1].set(c_ctx)
    mod = _modulation(s_in, w_mod, b_mod)
    rope_a = _rope_tables(n_lat, A_HEAD_DIM)
    rope_c = _rope_tables(n_lat, C_HEAD_DIM)
    ev_w_in_b, ev_w_out_b = ev_w_in.astype(BF16), ev_w_out.astype(BF16)
    od_w_in_b, od_w_out_b = od_w_in.astype(BF16), od_w_out.astype(BF16)

    for i in range(DEPTH):
        final = i == DEPTH - 1
        n_rows = n_lat if final else t
        tm = OUT_TM
        if i % 2 == 0:
            e = i // 2
            lam_init = 0.8 - 0.6 * math.exp(-0.3 * i)
            p = _proj(xs, mod[i], norm_g[i], ev_w_in_b[e], *rope_a, qk_norm[0], n_lat=n_lat,
                      even=True)
            ya = _diff_attention(p, diff_lambda[e], diff_subln[e], None, lam_init=lam_init,
                                 q_row0=0, n_q=n_lat, k_row0=0, n_k=t)
            if not final:
                ya = _diff_attention(p, diff_lambda[e], diff_subln[e], ya, lam_init=lam_init,
                                     q_row0=n_lat, n_q=CTX_LEN, k_row0=n_lat, n_k=CTX_LEN)
            o_f, o_b = _retention(p, ret_decay[e], n_lat=n_lat)
            bw = B_WIDTH
            args = [ya, o_f, o_b, p, ret_gn[e].reshape(1, bw)]
            specs = [pl.BlockSpec((tm, A_WIDTH), lambda r: (r, 0)),
                     pl.BlockSpec((tm, bw), lambda r: (r, 0)),
                     pl.BlockSpec((tm, bw), lambda r: (r, 0)),
                     pl.BlockSpec((tm, bw), lambda r: (r, (EV_IN - bw) // bw)),
                     pl.BlockSpec((1, bw), lambda r: (0, 0))]
            xs = _out_proj(xs, args, specs, ev_w_out_b[e], mod[i], final_norm, n_lat=n_lat,
                           n_rows=n_rows, even=True, final=final)
        else:
            o = i // 2
            p = _proj(xs, mod[i], norm_g[i], od_w_in_b[o], *rope_c, qk_norm[o], n_lat=n_lat,
                      even=False)
            y = _gqa_attention(p, None, q_row0=0, n_q=n_lat, k_row0=0, n_k=t)
            if not final:
                y = _gqa_attention(p, y, q_row0=n_lat, n_q=CTX_LEN, k_row0=n_lat, n_k=CTX_LEN)
            specs = [pl.BlockSpec((tm, d), lambda r: (r, 0))]
            xs = _out_proj(xs, [y], specs, od_w_out_b[o], mod[i], final_norm, n_lat=n_lat,
                           n_rows=n_rows, even=False, final=final)
    return xs[None]
```

```python
import functools
import math

import jax
import jax.numpy as jnp
from jax import lax
from jax.experimental import pallas as pl
from jax.experimental.pallas import tpu as pltpu

F32 = jnp.float32
BF16 = jnp.bfloat16

D_MODEL = 2048
DEPTH = 4
GRID_W = 64
CTX_LEN = 256
RET_CHUNK = 128
ROPE_THETA = 10000.0
EPS = 1e-6

A_WIDTH = D_MODEL // 2
A_HEADS = 8
A_HEAD_DIM = A_WIDTH // A_HEADS // 2
A_V_DIM = 2 * A_HEAD_DIM
B_WIDTH = D_MODEL - A_WIDTH
B_HEADS = 4
B_V_DIM = B_WIDTH // B_HEADS
B_K_DIM = B_V_DIM // 2
EV_IN = 7168

C_HEADS = 16
C_KV_HEADS = 4
C_GROUP = C_HEADS // C_KV_HEADS
C_HEAD_DIM = D_MODEL // C_HEADS
OD_IN = 5120

LANES = 128
SUBLANES = 8
MXU_TILE = 256
ONES_ROWS = 16
SCORE_ROWS = 256
SOFTMAX_ROWS = 32
MAX_CHAINS = 4
LOG2E = math.log2(math.e)
VMEM_LIMIT_BYTES = 56 * 1024 * 1024
PROJ_TM = 1280
PROJ_TN = 512
NORM_ROWS = 256
OUT_TM = 256
KV_CHUNK = 1280
ATT_ROWS = 4096


def _silu(x):
    return x / (1.0 + jnp.exp(-x))


def _cparams(sem):
    return pltpu.CompilerParams(dimension_semantics=sem, vmem_limit_bytes=VMEM_LIMIT_BYTES)


def _mod_kernel(s_ref, w_ref, b_ref, o_ref):
    s = _silu(s_ref[...])
    acc = jnp.dot(s.astype(BF16), w_ref[0].astype(BF16), preferred_element_type=F32)
    o_ref[0] = acc + b_ref[0]


def _modulation(s_in, w_mod, b_mod):
    depth, d, n = w_mod.shape
    tn = 768
    return pl.pallas_call(
        _mod_kernel,
        grid=(depth, n // tn),
        in_specs=[
            pl.BlockSpec((8, d), lambda l, j: (0, 0)),
            pl.BlockSpec((1, d, tn), lambda l, j: (l, 0, j)),
            pl.BlockSpec((1, 1, tn), lambda l, j: (l, 0, j)),
        ],
        out_specs=pl.BlockSpec((1, 8, tn), lambda l, j: (l, 0, j)),
        out_shape=jax.ShapeDtypeStruct((depth, 8, n), F32),
        compiler_params=_cparams(("parallel", "parallel")),
        name="modulation",
    )(s_in, w_mod, b_mod.reshape(depth, 1, n))


def _rope(x, cos, sin, quarter):
    lane = lax.broadcasted_iota(jnp.int32, x.shape, 1)
    first = (lane & quarter) == 0
    partner = jnp.where(first, pltpu.roll(x, LANES - quarter, 1), pltpu.roll(x, quarter, 1))
    return x * cos + partner * sin


def _proj_kernel(x_ref, mod_ref, g_ref, w_ref, cos_ref, sin_ref, qk_ref, o_ref, h_ref, *,
                 n_lat, tm, tn, even):
    i = pl.program_id(0)
    j = pl.program_id(1)
    d = x_ref.shape[1]

    @pl.when(j == 0)
    def _():
        def norm_rows(r, carry):
            r0 = pl.multiple_of(r * NORM_ROWS, NORM_ROWS)
            x = x_ref[pl.ds(r0, NORM_ROWS), :]
            ms = jnp.mean(x * x, axis=-1, keepdims=True)
            y = x * lax.rsqrt(ms + EPS) * g_ref[...]
            row = i * tm + r0 + lax.broadcasted_iota(jnp.int32, (NORM_ROWS, 1), 0)
            is_ctx = row >= n_lat
            shift = jnp.where(is_ctx, mod_ref[1:2, 0:d], mod_ref[0:1, 0:d])
            scale = jnp.where(is_ctx, mod_ref[1:2, d:2 * d], mod_ref[0:1, d:2 * d])
            h_ref[pl.ds(r0, NORM_ROWS), :] = (y * (1.0 + scale) + shift).astype(BF16)
            return carry

        lax.fori_loop(0, tm // NORM_ROWS, norm_rows, 0)

    acc = jnp.dot(h_ref[...], w_ref[...], preferred_element_type=F32)
    groups = tn // LANES

    if even:
        n_rope = 2 * (2 * A_HEADS * A_HEAD_DIM) // tn
        n_q = n_rope // 2
        bk_tile = (4 * A_WIDTH + B_HEADS * B_K_DIM) // tn

        @pl.when(j < n_rope)
        def _():
            cos = cos_ref[...]
            sin = sin_ref[...]
            qscale = jnp.where(j < n_q, A_HEAD_DIM ** -0.5 * LOG2E, 1.0).astype(F32)
            for g in range(groups):
                sl = slice(g * LANES, (g + 1) * LANES)
                o_ref[:, sl] = (_rope(acc[:, sl], cos, sin, A_HEAD_DIM // 4) * qscale).astype(BF16)

        @pl.when(j == bk_tile)
        def _():
            o_ref[...] = (acc * (B_K_DIM ** -0.5)).astype(BF16)

        @pl.when(jnp.logical_and(j >= n_rope, j != bk_tile))
        def _():
            o_ref[...] = acc.astype(BF16)
    else:
        n_q = (C_HEADS * C_HEAD_DIM) // tn
        n_qk = n_q + (C_KV_HEADS * C_HEAD_DIM) // tn

        @pl.when(j < n_qk)
        def _():
            cos = cos_ref[...]
            sin = sin_ref[...]
            is_q = j < n_q
            gain = jnp.where(is_q, qk_ref[0:1, :], qk_ref[1:2, :])
            qscale = jnp.where(is_q, C_HEAD_DIM ** -0.5 * LOG2E, 1.0).astype(F32)
            for g in range(groups):
                sl = slice(g * LANES, (g + 1) * LANES)
                xg = acc[:, sl]
                ms = jnp.mean(xg * xg, axis=-1, keepdims=True)
                yg = xg * lax.rsqrt(ms + EPS) * gain
                o_ref[:, sl] = (_rope(yg, cos, sin, C_HEAD_DIM // 4) * qscale).astype(BF16)

        @pl.when(j >= n_qk)
        def _():
            o_ref[...] = acc.astype(BF16)


def _proj(xs, mod_l, norm_g, w_in, cos, sin, qk_g, *, n_lat, even):
    t, d = xs.shape
    n = w_in.shape[1]
    tm, tn = PROJ_TM, PROJ_TN
    kern = functools.partial(_proj_kernel, n_lat=n_lat, tm=tm, tn=tn, even=even)
    return pl.pallas_call(
        kern,
        grid=(t // tm, n // tn),
        in_specs=[
            pl.BlockSpec((tm, d), lambda i, j: (i, 0)),
            pl.BlockSpec(mod_l.shape, lambda i, j: (0, 0)),
            pl.BlockSpec((1, d), lambda i, j: (0, 0)),
            pl.BlockSpec((d, tn), lambda i, j: (0, j)),
            pl.BlockSpec((tm, LANES), lambda i, j: (i, 0)),
            pl.BlockSpec((tm, LANES), lambda i, j: (i, 0)),
            pl.BlockSpec(qk_g.shape, lambda i, j: (0, 0)),
        ],
        out_specs=pl.BlockSpec((tm, tn), lambda i, j: (i, j)),
        out_shape=jax.ShapeDtypeStruct((t, n), BF16),
        scratch_shapes=[pltpu.VMEM((tm, d), BF16)],
        compiler_params=_cparams(("parallel", "arbitrary")),
        name="proj_even" if even else "proj_odd",
    )(xs, mod_l, norm_g.reshape(1, d), w_in, cos, sin, qk_g)


def _flash_t(qs, qs_next, k_ref, vt_ref, s_scr, mx_scr, m_scr, acc_scr, ck, n_chunks, tile):
    sub = SUBLANES
    n_col = qs.shape[0] // MXU_TILE

    def scores(q, c, nt):
        cols = slice(nt * MXU_TILE, (nt + 1) * MXU_TILE)
        parts = None
        for kb in range(ck // SCORE_ROWS):
            rows = pl.ds(pl.multiple_of(c * ck + kb * SCORE_ROWS, SCORE_ROWS), SCORE_ROWS)
            s = lax.dot_general(k_ref[rows, :], q[cols], (((1,), (1,)), ((), ())),
                                preferred_element_type=F32)
            s_scr[kb * SCORE_ROWS:(kb + 1) * SCORE_ROWS, cols] = s
            blk = [s[t * sub:(t + 1) * sub] for t in range(SCORE_ROWS // sub)]
            if parts is None:
                parts, blk = blk[:MAX_CHAINS], blk[MAX_CHAINS:]
            for r, b in enumerate(blk):
                parts[r % MAX_CHAINS] = jnp.maximum(parts[r % MAX_CHAINS], b)
        mx_scr[:, cols] = functools.reduce(jnp.maximum, parts)

    def softmax_pv(c, nt):
        cols = slice(nt * MXU_TILE, (nt + 1) * MXU_TILE)
        m_prev = m_scr[:, cols]
        m_new = jnp.maximum(m_prev, jnp.max(mx_scr[:, cols], axis=0, keepdims=True))
        alpha = jnp.exp2(m_prev - m_new)
        m_scr[:, cols] = m_new
        m_blk = jnp.broadcast_to(m_new, (SOFTMAX_ROWS, MXU_TILE))
        acc = alpha * acc_scr[:, cols]
        blocks = []
        for r in range(ck // SOFTMAX_ROWS):
            r0 = r * SOFTMAX_ROWS
            pb = jnp.exp2(s_scr[r0:r0 + SOFTMAX_ROWS, cols] - m_blk)
            blocks.append(pb.astype(BF16))
        p_col = jnp.concatenate(blocks, axis=0)
        acc_scr[:, cols] = acc + jnp.dot(vt_ref[c], p_col, preferred_element_type=F32)

    def step(c, q_following, c_following):
        for nt in range(n_col):
            softmax_pv(c, nt)
            scores(q_following, c_following, nt)

    m_scr[...] = jnp.full(m_scr.shape, -jnp.inf, F32)
    acc_scr[...] = jnp.zeros(acc_scr.shape, F32)

    @pl.when(tile == 0)
    def _():
        for nt in range(n_col):
            scores(qs, 0, nt)

    def one(c, carry):
        step(c, qs, c + 1)
        return carry

    lax.fori_loop(0, n_chunks - 1, one, 0)
    step(n_chunks - 1, qs_next, 0)
    dv = acc_scr.shape[0] - ONES_ROWS
    return acc_scr[0:dv, :], acc_scr[dv:dv + 1, :]


def _flash_scratch(m_rows, ck, dv):
    return [pltpu.VMEM((ck, m_rows), F32), pltpu.VMEM((SUBLANES, m_rows), F32),
            pltpu.VMEM((1, m_rows), F32), pltpu.VMEM((dv + ONES_ROWS, m_rows), F32)]


def _v_transposed(p, col0, heads, dv, row0, n_k, ck):
    v = p[row0:row0 + n_k, col0:col0 + heads * dv]
    vt = v.reshape(n_k // ck, ck, heads, dv).transpose(2, 0, 3, 1)
    extra = jnp.zeros(vt.shape[:2] + (ONES_ROWS, ck), vt.dtype).at[:, :, 0, :].set(1)
    return jnp.concatenate([vt, extra], axis=2)


def _gqa_kernel(q_ref, qn_ref, k_ref, vt_ref, g_ref, o_ref, *scratch, tq, ck, n_chunks):
    hd = C_HEAD_DIM
    stack = lambda ref: jnp.concatenate([ref[:, h * hd:(h + 1) * hd] for h in range(C_GROUP)],
                                        axis=0)
    acc, l = _flash_t(stack(q_ref), stack(qn_ref), k_ref, vt_ref, *scratch, ck, n_chunks,
                      pl.program_id(1))
    o_t = acc / l
    for h in range(C_GROUP):
        gate = g_ref[:, h * hd:(h + 1) * hd].astype(F32)
        o = o_t[:, h * tq:(h + 1) * tq].T
        o_ref[:, h * hd:(h + 1) * hd] = (o * _silu(gate)).astype(BF16)


def _skip_first_input(kern, *refs):
    return kern(*refs[1:])


def _gqa_attention(p, y_prev, *, q_row0, n_q, k_row0, n_k):
    t = p.shape[0]
    hd = C_HEAD_DIM
    tq = min(ATT_ROWS // C_GROUP, n_q)
    ck = min(KV_CHUNK, n_k)
    n_chunks = n_k // ck
    qb0, kb0 = q_row0 // tq, k_row0 // n_k
    w = C_GROUP * hd
    k_col0 = C_HEADS
    v_col0 = (C_HEADS + C_KV_HEADS) * hd
    g_col0 = (C_HEADS + 2 * C_KV_HEADS) * hd // w
    vt = _v_transposed(p, v_col0, C_KV_HEADS, hd, k_row0, n_k, ck)
    kern = functools.partial(_gqa_kernel, tq=tq, ck=ck, n_chunks=n_chunks)
    last = n_q // tq - 1
    in_specs = [
        pl.BlockSpec((tq, w), lambda h, i: (qb0 + i, h)),
        pl.BlockSpec((tq, w), lambda h, i: (qb0 + jnp.minimum(i + 1, last), h)),
        pl.BlockSpec((n_k, hd), lambda h, i: (kb0, k_col0 + h)),
        pl.BlockSpec((None, n_chunks, hd + ONES_ROWS, ck), lambda h, i: (h, 0, 0, 0)),
        pl.BlockSpec((tq, w), lambda h, i: (qb0 + i, g_col0 + h)),
    ]
    args = [p, p, p, vt, p]
    aliases = {}
    if y_prev is not None:
        in_specs.insert(0, pl.BlockSpec(memory_space=pl.ANY))
        args.insert(0, y_prev)
        aliases = {0: 0}
        kern = functools.partial(_skip_first_input, kern)
    return pl.pallas_call(
        kern,
        grid=(C_KV_HEADS, n_q // tq),
        in_specs=in_specs,
        out_specs=pl.BlockSpec((tq, w), lambda h, i: (qb0 + i, h)),
        out_shape=jax.ShapeDtypeStruct((t, D_MODEL), BF16),
        scratch_shapes=_flash_scratch(C_GROUP * tq, ck, hd),
        input_output_aliases=aliases,
        compiler_params=_cparams(("parallel", "arbitrary")),
        name="gqa_attention",
    )(*args)


def _diff_kernel(lam_ref, sub_ref, q_ref, qn_ref, k_ref, vt_ref, g_ref, o_ref, *scratch, tq,
                 ck, n_chunks, lam_init):
    def stack(ref):
        q = ref[...]
        lane = lax.broadcasted_iota(jnp.int32, q.shape, 1)
        zero = jnp.zeros_like(q)
        return jnp.concatenate([jnp.where(lane < A_HEAD_DIM, q, zero),
                                jnp.where(lane >= A_HEAD_DIM, q, zero)], axis=0)

    acc, l = _flash_t(stack(q_ref), stack(qn_ref), k_ref, vt_ref, *scratch, ck, n_chunks,
                      pl.program_id(1))
    o_t = acc / l
    lp = lam_ref[...]
    lam = (jnp.exp(jnp.sum(lp[0:1] * lp[1:2], axis=-1, keepdims=True))
           - jnp.exp(jnp.sum(lp[2:3] * lp[3:4], axis=-1, keepdims=True)) + lam_init)
    od = (o_t[:, :tq] - lam * o_t[:, tq:]).T
    ms = jnp.mean(od * od, axis=-1, keepdims=True)
    y = od * lax.rsqrt(ms + EPS) * sub_ref[...] * (1.0 - lam_init)
    o_ref[...] = (y * _silu(g_ref[...].astype(F32))).astype(BF16)


def _diff_attention(p, lam_p, subln, y_prev, *, lam_init, q_row0, n_q, k_row0, n_k):
    t = p.shape[0]
    tq = min(ATT_ROWS // 2, n_q)
    ck = min(KV_CHUNK, n_k)
    n_chunks = n_k // ck
    qb0, kb0 = q_row0 // tq, k_row0 // n_k
    w = A_V_DIM
    last = n_q // tq - 1
    vt = _v_transposed(p, 2 * A_HEADS * w, A_HEADS, w, k_row0, n_k, ck)
    kern = functools.partial(_diff_kernel, tq=tq, ck=ck, n_chunks=n_chunks, lam_init=lam_init)
    in_specs = [
        pl.BlockSpec(lam_p.shape, lambda h, i: (0, 0)),
        pl.BlockSpec((1, w), lambda h, i: (0, 0)),
        pl.BlockSpec((tq, w), lambda h, i: (qb0 + i, h)),
        pl.BlockSpec((tq, w), lambda h, i: (qb0 + jnp.minimum(i + 1, last), h)),
        pl.BlockSpec((n_k, w), lambda h, i: (kb0, A_HEADS + h)),
        pl.BlockSpec((None, n_chunks, w + ONES_ROWS, ck), lambda h, i: (h, 0, 0, 0)),
        pl.BlockSpec((tq, w), lambda h, i: (qb0 + i, 3 * A_HEADS + h)),
    ]
    args = [lam_p, subln.reshape(1, w), p, p, p, vt, p]
    aliases = {}
    if y_prev is not None:
        in_specs.insert(0, pl.BlockSpec(memory_space=pl.ANY))
        args.insert(0, y_prev)
        aliases = {0: 0}
        kern = functools.partial(_skip_first_input, kern)
    return pl.pallas_call(
        kern,
        grid=(A_HEADS, n_q // tq),
        in_specs=in_specs,
        out_specs=pl.BlockSpec((tq, w), lambda h, i: (qb0 + i, h)),
        out_shape=jax.ShapeDtypeStruct((t, A_WIDTH), BF16),
        scratch_shapes=_flash_scratch(2 * tq, ck, w),
        input_output_aliases=aliases,
        compiler_params=_cparams(("parallel", "arbitrary")),
        name="diff_attention",
    )(*args)


def _ret_kernel(a_ref, qf_ref, kf_ref, vf_ref, qb_ref, kb_ref, vb_ref, of_ref, ob_ref,
                s_ref, dec_ref, qd_ref, kd_ref, cd_ref):
    n = pl.program_id(0)
    c = RET_CHUNK

    @pl.when(n == 0)
    def _():
        i = lax.broadcasted_iota(jnp.int32, (c, c), 0).astype(F32)
        j = lax.broadcasted_iota(jnp.int32, (c, c), 1).astype(F32)
        for d in range(2):
            for h in range(B_HEADS):
                ch = d * B_HEADS + h
                lg = -jnp.exp(a_ref[ch:ch + 1, :])
                rel = (i - j) if d == 0 else (j - i)
                dec_ref[ch] = jnp.where(rel >= 0, jnp.exp(lg * jnp.maximum(rel, 0.0)), 0.0)
                qd_ref[ch] = jnp.exp(lg * ((i + 1.0) if d == 0 else (c - i)))
                kd_ref[ch] = jnp.exp(lg * ((c - 1.0 - i) if d == 0 else i))
                cd_ref[ch] = jnp.exp(lg * float(c)) + jnp.zeros((8, LANES), F32)
                s_ref[ch] = jnp.zeros(s_ref.shape[1:], F32)

    for d, (q_ref, k_ref, v_ref, o_ref) in enumerate(((qf_ref, kf_ref, vf_ref, of_ref),
                                                      (qb_ref, kb_ref, vb_ref, ob_ref))):
        for h in range(B_HEADS):
            ch = d * B_HEADS + h
            q = q_ref[:, h * B_K_DIM:(h + 1) * B_K_DIM]
            k = k_ref[:, h * B_K_DIM:(h + 1) * B_K_DIM]
            v = v_ref[:, h * B_V_DIM:(h + 1) * B_V_DIM]
            st = s_ref[ch]
            sc = lax.dot_general(q, k, (((1,), (1,)), ((), ())),
                                 preferred_element_type=F32) * dec_ref[ch]
            intra = jnp.dot(sc.astype(BF16), v, preferred_element_type=F32)
            qdec = (q.astype(F32) * qd_ref[ch]).astype(BF16)
            cross = jnp.dot(qdec, st.astype(BF16), preferred_element_type=F32)
            o_ref[:, h * B_V_DIM:(h + 1) * B_V_DIM] = intra + cross
            kdec = (k.astype(F32) * kd_ref[ch]).T.astype(BF16)
            upd = jnp.dot(kdec, v, preferred_element_type=F32)
            s_ref[ch] = st * cd_ref[ch][0:1, 0:1] + upd


def _retention(p, ret_decay, *, n_lat):
    t = p.shape[0]
    c = RET_CHUNK
    nc = t // c
    n_lat_c = n_lat // c
    qw, vw = B_HEADS * B_K_DIM, B_HEADS * B_V_DIM
    q_col = 4 * A_WIDTH // qw
    k_col = q_col + 1
    v_col = (4 * A_WIDTH + 2 * qw) // vw
    fwd = lambda n: (n + n_lat_c) % nc
    bwd = lambda n: nc - 1 - n
    return pl.pallas_call(
        _ret_kernel,
        grid=(nc,),
        in_specs=[
            pl.BlockSpec((2 * B_HEADS, 1), lambda n: (0, 0)),
            pl.BlockSpec((c, qw), lambda n: (fwd(n), q_col)),
            pl.BlockSpec((c, qw), lambda n: (fwd(n), k_col)),
            pl.BlockSpec((c, vw), lambda n: (fwd(n), v_col)),
            pl.BlockSpec((c, qw), lambda n: (bwd(n), q_col)),
            pl.BlockSpec((c, qw), lambda n: (bwd(n), k_col)),
            pl.BlockSpec((c, vw), lambda n: (bwd(n), v_col)),
        ],
        out_specs=[
            pl.BlockSpec((c, vw), lambda n: (fwd(n), 0)),
            pl.BlockSpec((c, vw), lambda n: (bwd(n), 0)),
        ],
        out_shape=[jax.ShapeDtypeStruct((t, vw), F32), jax.ShapeDtypeStruct((t, vw), F32)],
        scratch_shapes=[
            pltpu.VMEM((2 * B_HEADS, B_K_DIM, B_V_DIM), F32),
            pltpu.VMEM((2 * B_HEADS, c, c), F32),
            pltpu.VMEM((2 * B_HEADS, c, B_K_DIM), F32),
            pltpu.VMEM((2 * B_HEADS, c, B_K_DIM), F32),
            pltpu.VMEM((2 * B_HEADS, 8, LANES), F32),
        ],
        compiler_params=_cparams(("arbitrary",)),
        name="retention",
    )(ret_decay.reshape(2 * B_HEADS, 1), p, p, p, p, p, p)


def _out_tail(x_ref, y, mod_ref, fin_ref, o_ref, *, n_lat, tm, final):
    d = x_ref.shape[1]
    row = pl.program_id(0) * tm + lax.broadcasted_iota(jnp.int32, (tm, 1), 0)
    gate = jnp.where(row >= n_lat, mod_ref[1:2, 2 * d:3 * d], mod_ref[0:1, 2 * d:3 * d])
    xn = x_ref[...] + gate * y
    if final:
        ms = jnp.mean(xn * xn, axis=-1, keepdims=True)
        xn = xn * lax.rsqrt(ms + EPS) * fin_ref[...]
    o_ref[...] = xn


def _out_even_kernel(x_ref, ya_ref, of_ref, ob_ref, bg_ref, gn_ref, w_ref, mod_ref, fin_ref,
                     o_ref, *, n_lat, tm, final):
    ob = of_ref[...] + ob_ref[...]
    gate = _silu(bg_ref[...].astype(F32))
    parts = []
    for h in range(B_HEADS):
        sl = slice(h * B_V_DIM, (h + 1) * B_V_DIM)
        z = ob[:, sl]
        zc = z - jnp.mean(z, axis=-1, keepdims=True)
        yh = zc * lax.rsqrt(jnp.mean(zc * zc, axis=-1, keepdims=True) + EPS) * gn_ref[:, sl]
        parts.append((yh * gate[:, sl]).astype(BF16))
    yb = jnp.concatenate(parts, axis=1)
    ka = ya_ref.shape[1]
    y = (jnp.dot(ya_ref[...], w_ref[0:ka, :], preferred_element_type=F32)
         + jnp.dot(yb, w_ref[ka:, :], preferred_element_type=F32))
    _out_tail(x_ref, y, mod_ref, fin_ref, o_ref, n_lat=n_lat, tm=tm, final=final)


def _out_odd_kernel(x_ref, y_ref, w_ref, mod_ref, fin_ref, o_ref, *, n_lat, tm, final):
    y = jnp.dot(y_ref[...], w_ref[...], preferred_element_type=F32)
    _out_tail(x_ref, y, mod_ref, fin_ref, o_ref, n_lat=n_lat, tm=tm, final=final)


def _out_proj(xs, mixer_args, mixer_specs, w_out, mod_l, final_norm, *, n_lat, n_rows, even,
              final):
    t, d = xs.shape
    tm = OUT_TM
    body = _out_even_kernel if even else _out_odd_kernel
    kern = functools.partial(body, n_lat=n_lat, tm=tm, final=final)
    row_spec = pl.BlockSpec((tm, d), lambda i: (i, 0))
    const = lambda shape: pl.BlockSpec(shape, lambda i: (0, 0))
    return pl.pallas_call(
        kern,
        grid=(n_rows // tm,),
        in_specs=[row_spec] + mixer_specs + [const(w_out.shape), const(mod_l.shape), const((1, d))],
        out_specs=row_spec,
        out_shape=jax.ShapeDtypeStruct((n_rows, d), F32),
        compiler_params=_cparams(("parallel",)),
        name="out_proj",
    )(xs, *mixer_args, w_out, mod_l, final_norm.reshape(1, d))


def _rope_tables(n_lat, head_dim):
    pos = jnp.arange(n_lat, dtype=jnp.int32)
    rows = (pos // GRID_W).astype(F32)[:, None]
    cols = (pos % GRID_W).astype(F32)[:, None]
    axis_dim = head_dim // 2
    inv = ROPE_THETA ** (-jnp.arange(0, axis_dim, 2, dtype=F32) / axis_dim)
    ar, ac = rows * inv, cols * inv
    cos = jnp.concatenate([jnp.cos(ar), jnp.cos(ar), jnp.cos(ac), jnp.cos(ac)], axis=-1)
    sin = jnp.concatenate([-jnp.sin(ar), jnp.sin(ar), -jnp.sin(ac), jnp.sin(ac)], axis=-1)
    reps = LANES // head_dim
    cos, sin = jnp.tile(cos, (1, reps)), jnp.tile(sin, (1, reps))
    cos = jnp.concatenate([cos, jnp.ones((CTX_LEN, LANES), F32)], axis=0)
    sin = jnp.concatenate([sin, jnp.zeros((CTX_LEN, LANES), F32)], axis=0)
    return cos, sin


def kernel(x, c, ctx, c_ctx, norm_g, w_mod, b_mod, ev_w_in, ev_w_out, diff_lambda, diff_subln,
           ret_decay, ret_gn, od_w_in, od_w_out, qk_norm, final_norm):
    assert x.shape[0] == 1 and ctx.shape[1] == CTX_LEN and x.shape[2] == D_MODEL
    n_lat = x.shape[1]
    t = n_lat + CTX_LEN
    d = D_MODEL
    assert t % PROJ_TM == 0 and t % KV_CHUNK == 0 and n_lat % (ATT_ROWS // 2) == 0

    xs = jnp.concatenate([x[0], ctx[0]], axis=0)
    s_in = jnp.zeros((8, d), F32).at[0].set(c[0]).at[1].set(c_ctx)
    mod = _modulation(s_in, w_mod, b_mod)
    rope_a = _rope_tables(n_lat, A_HEAD_DIM)
    rope_c = _rope_tables(n_lat, C_HEAD_DIM)
    ev_w_in_b, ev_w_out_b = ev_w_in.astype(BF16), ev_w_out.astype(BF16)
    od_w_in_b, od_w_out_b = od_w_in.astype(BF16), od_w_out.astype(BF16)

    for i in range(DEPTH):
        final = i == DEPTH - 1
        n_rows = n_lat if final else t
        tm = OUT_TM
        if i % 2 == 0:
            e = i // 2
            lam_init = 0.8 - 0.6 * math.exp(-0.3 * i)
            p = _proj(xs, mod[i], norm_g[i], ev_w_in_b[e], *rope_a, qk_norm[0], n_lat=n_lat,
                      even=True)
            ya = _diff_attention(p, diff_lambda[e], diff_subln[e], None, lam_init=lam_init,
                                 q_row0=0, n_q=n_lat, k_row0=0, n_k=t)
            if not final:
                ya = _diff_attention(p, diff_lambda[e], diff_subln[e], ya, lam_init=lam_init,
                                     q_row0=n_lat, n_q=CTX_LEN, k_row0=n_lat, n_k=CTX_LEN)
            o_f, o_b = _retention(p, ret_decay[e], n_lat=n_lat)
            bw = B_WIDTH
            args = [ya, o_f, o_b, p, ret_gn[e].reshape(1, bw)]
            specs = [pl.BlockSpec((tm, A_WIDTH), lambda r: (r, 0)),
                     pl.BlockSpec((tm, bw), lambda r: (r, 0)),
                     pl.BlockSpec((tm, bw), lambda r: (r, 0)),
                     pl.BlockSpec((tm, bw), lambda r: (r, (EV_IN - bw) // bw)),
                     pl.BlockSpec((1, bw), lambda r: (0, 0))]
            xs = _out_proj(xs, args, specs, ev_w_out_b[e], mod[i], final_norm, n_lat=n_lat,
                           n_rows=n_rows, even=True, final=final)
        else:
            o = i // 2
            p = _proj(xs, mod[i], norm_g[i], od_w_in_b[o], *rope_c, qk_norm[o], n_lat=n_lat,
                      even=False)
            y = _gqa_attention(p, None, q_row0=0, n_q=n_lat, k_row0=0, n_k=t)
            if not final:
                y = _gqa_attention(p, y, q_row0=n_lat, n_q=CTX_LEN, k_row0=n_lat, n_k=CTX_LEN)
            specs = [pl.BlockSpec((tm, d), lambda r: (r, 0))]
            xs = _out_proj(xs, [y], specs, od_w_out_b[o], mod[i], final_norm, n_lat=n_lat,
                           n_rows=n_rows, even=False, final=final)
    return xs[None]
```

```python
import functools
import math

import jax
import jax.numpy as jnp
from jax import lax
from jax.experimental import pallas as pl
from jax.experimental.pallas import tpu as pltpu

F32 = jnp.float32
BF16 = jnp.bfloat16

D_MODEL = 2048
DEPTH = 4
GRID_W = 64
CTX_LEN = 256
RET_CHUNK = 128
ROPE_THETA = 10000.0
EPS = 1e-6

A_WIDTH = D_MODEL // 2
A_HEADS = 8
A_HEAD_DIM = A_WIDTH // A_HEADS // 2
A_V_DIM = 2 * A_HEAD_DIM
B_WIDTH = D_MODEL - A_WIDTH
B_HEADS = 4
B_V_DIM = B_WIDTH // B_HEADS
B_K_DIM = B_V_DIM // 2
A_QK = 2 * A_HEADS * A_HEAD_DIM
EV_IN = 7168

C_HEADS = 16
C_KV_HEADS = 4
C_GROUP = C_HEADS // C_KV_HEADS
C_HEAD_DIM = D_MODEL // C_HEADS
OD_IN = 5120

LANES = 128
SUBLANES = 8
MXU_TILE = 256
ONES_ROWS = 16
SCORE_ROWS = 256
SOFTMAX_ROWS = 32
MAX_CHAINS = 4
LOG2E = math.log2(math.e)
VMEM_LIMIT_BYTES = 56 * 1024 * 1024
PROJ_TM = 1280
PROJ_TN = 512
NORM_ROWS = 64
NORM_SUB = 16
OUT_TM = 256
KV_CHUNK = 1280
ATT_ROWS = 4096


def _silu(x):
    return x / (1.0 + jnp.exp(-x))


def _cparams(sem):
    return pltpu.CompilerParams(dimension_semantics=sem, vmem_limit_bytes=VMEM_LIMIT_BYTES)


def _mod_kernel(s_ref, w_ref, b_ref, o_ref):
    s = _silu(s_ref[...])
    acc = jnp.dot(s.astype(BF16), w_ref[0].astype(BF16), preferred_element_type=F32)
    o_ref[0] = acc + b_ref[0]


def _modulation(s_in, w_mod, b_mod):
    depth, d, n = w_mod.shape
    tn = 768
    return pl.pallas_call(
        _mod_kernel,
        grid=(depth, n // tn),
        in_specs=[
            pl.BlockSpec((8, d), lambda l, j: (0, 0)),
            pl.BlockSpec((1, d, tn), lambda l, j: (l, 0, j)),
            pl.BlockSpec((1, 1, tn), lambda l, j: (l, 0, j)),
        ],
        out_specs=pl.BlockSpec((1, 8, tn), lambda l, j: (l, 0, j)),
        out_shape=jax.ShapeDtypeStruct((depth, 8, n), F32),
        compiler_params=_cparams(("parallel", "parallel")),
        name="modulation",
    )(s_in, w_mod, b_mod.reshape(depth, 1, n))


def _rope(x, cos, sin):
    return x * cos + pltpu.roll(x, LANES // 2, 1) * sin


def _proj_kernel(x_ref, mod_ref, g_ref, w_ref, cos_ref, sin_ref, qk_ref, o_ref, vt_ref, h_ref, *,
                 n_lat, tm, tn, even, v_tile0, n_v_tiles):
    i = pl.program_id(0)
    j = pl.program_id(1)
    d = x_ref.shape[1]

    @pl.when(j == 0)
    def _():
        gain = [g_ref[...] * (1.0 + mod_ref[k:k + 1, d:2 * d]) for k in range(2)]
        shift = [mod_ref[k:k + 1, 0:d] for k in range(2)]

        def norm_rows(r, carry):
            r0 = pl.multiple_of(r * NORM_ROWS, NORM_ROWS)
            is_ctx = i * tm + r0 >= n_lat
            gm = jnp.where(is_ctx, gain[1], gain[0])
            sh = jnp.where(is_ctx, shift[1], shift[0])
            for u in range(NORM_ROWS // NORM_SUB):
                rows = pl.ds(r0 + u * NORM_SUB, NORM_SUB)
                x = x_ref[rows, :]
                ms = jnp.mean(x * x, axis=-1, keepdims=True)
                h_ref[rows, :] = (x * lax.rsqrt(ms + EPS) * gm + sh).astype(BF16)
            return carry

        lax.fori_loop(0, tm // NORM_ROWS, norm_rows, 0)

    acc = jnp.dot(h_ref[...], w_ref[...], preferred_element_type=F32)
    groups = tn // LANES

    @pl.when(jnp.logical_and(j >= v_tile0, j < v_tile0 + n_v_tiles))
    def _():
        row = lax.broadcasted_iota(jnp.int32, (ONES_ROWS, tm), 0)
        ones = jnp.where(row == 0, 1.0, 0.0).astype(BF16)
        for g in range(groups):
            vt_ref[g, 0:LANES, :] = acc[:, g * LANES:(g + 1) * LANES].T.astype(BF16)
            vt_ref[g, LANES:LANES + ONES_ROWS, :] = ones

    if even:
        n_rope = 2 * (2 * A_HEADS * A_HEAD_DIM) // tn
        n_q = n_rope // 2
        bk_tile = (4 * A_WIDTH + B_HEADS * B_K_DIM) // tn

        @pl.when(j < n_rope)
        def _():
            cos = cos_ref[...]
            sin = sin_ref[...]
            qscale = jnp.where(j < n_q, A_HEAD_DIM ** -0.5 * LOG2E, 1.0).astype(F32)
            for g in range(groups):
                sl = slice(g * LANES, (g + 1) * LANES)
                o_ref[:, sl] = (_rope(acc[:, sl], cos, sin) * qscale).astype(BF16)

        @pl.when(j == bk_tile)
        def _():
            o_ref[...] = (acc * (B_K_DIM ** -0.5)).astype(BF16)

        @pl.when(jnp.logical_and(j >= n_rope, j != bk_tile))
        def _():
            o_ref[...] = acc.astype(BF16)
    else:
        n_q = (C_HEADS * C_HEAD_DIM) // tn
        n_qk = n_q + (C_KV_HEADS * C_HEAD_DIM) // tn

        @pl.when(j < n_qk)
        def _():
            cos = cos_ref[...]
            sin = sin_ref[...]
            is_q = j < n_q
            gain = jnp.where(is_q, qk_ref[0:1, :], qk_ref[1:2, :])
            qscale = jnp.where(is_q, C_HEAD_DIM ** -0.5 * LOG2E, 1.0).astype(F32)
            for g in range(groups):
                sl = slice(g * LANES, (g + 1) * LANES)
                xg = acc[:, sl]
                ms = jnp.mean(xg * xg, axis=-1, keepdims=True)
                yg = xg * lax.rsqrt(ms + EPS) * gain
                o_ref[:, sl] = (_rope(yg, cos, sin) * qscale).astype(BF16)

        @pl.when(j >= n_qk)
        def _():
            o_ref[...] = acc.astype(BF16)


def _proj(xs, mod_l, norm_g, w_in, layer, cos, sin, qk_g, *, n_lat, even):
    t, d = xs.shape
    n = w_in.shape[2]
    tm, tn = PROJ_TM, PROJ_TN
    hpt = tn // LANES
    v_col0, v_heads = (2 * A_QK, A_HEADS) if even else (C_HEADS * C_HEAD_DIM + C_KV_HEADS * C_HEAD_DIM,
                                                        C_KV_HEADS)
    v_tile0, n_v_tiles = v_col0 // tn, v_heads // hpt
    kern = functools.partial(_proj_kernel, n_lat=n_lat, tm=tm, tn=tn, even=even, v_tile0=v_tile0,
                             n_v_tiles=n_v_tiles)
    vt_map = lambda i, j: (jnp.clip(j - v_tile0, 0, n_v_tiles - 1), i, 0, 0)
    return pl.pallas_call(
        kern,
        grid=(t // tm, n // tn),
        in_specs=[
            pl.BlockSpec((tm, d), lambda i, j: (i, 0)),
            pl.BlockSpec(mod_l.shape, lambda i, j: (0, 0)),
            pl.BlockSpec((1, d), lambda i, j: (0, 0)),
            pl.BlockSpec((None, d, tn), lambda i, j: (layer, 0, j)),
            pl.BlockSpec((tm, LANES), lambda i, j: (i, 0)),
            pl.BlockSpec((tm, LANES), lambda i, j: (i, 0)),
            pl.BlockSpec(qk_g.shape, lambda i, j: (0, 0)),
        ],
        out_specs=[pl.BlockSpec((tm, tn), lambda i, j: (i, j)),
                   pl.BlockSpec((hpt, None, LANES + ONES_ROWS, tm), vt_map)],
        out_shape=[jax.ShapeDtypeStruct((t, n), BF16),
                   jax.ShapeDtypeStruct((v_heads, t // tm, LANES + ONES_ROWS, tm), BF16)],
        scratch_shapes=[pltpu.VMEM((tm, d), BF16)],
        compiler_params=_cparams(("parallel", "arbitrary")),
        name="proj_even" if even else "proj_odd",
    )(xs, mod_l, norm_g.reshape(1, d), w_in, cos, sin, qk_g)


def _flash_t(qs, qs_next, k_ref, vt_ref, s_scr, mx_scr, m_scr, acc_scr, ck, n_chunks, tile):
    sub = SUBLANES
    n_col = qs.shape[0] // MXU_TILE
    score_rows = min(SCORE_ROWS, ck)

    def scores(q, c, nt):
        cols = slice(nt * MXU_TILE, (nt + 1) * MXU_TILE)
        parts = None
        for kb in range(ck // score_rows):
            rows = pl.ds(pl.multiple_of(c * ck + kb * score_rows, score_rows), score_rows)
            s = lax.dot_general(k_ref[rows, :], q[cols], (((1,), (1,)), ((), ())),
                                preferred_element_type=F32)
            s_scr[kb * score_rows:(kb + 1) * score_rows, cols] = s
            blk = [s[t * sub:(t + 1) * sub] for t in range(score_rows // sub)]
            if parts is None:
                parts, blk = blk[:MAX_CHAINS], blk[MAX_CHAINS:]
            for r, b in enumerate(blk):
                parts[r % MAX_CHAINS] = jnp.maximum(parts[r % MAX_CHAINS], b)
        mx_scr[:, cols] = functools.reduce(jnp.maximum, parts)

    def softmax_pv(c, nt):
        cols = slice(nt * MXU_TILE, (nt + 1) * MXU_TILE)
        m_prev = m_scr[:, cols]
        m_new = jnp.maximum(m_prev, jnp.max(mx_scr[:, cols], axis=0, keepdims=True))
        alpha = jnp.exp2(m_prev - m_new)
        m_scr[:, cols] = m_new
        m_blk = jnp.broadcast_to(m_new, (SOFTMAX_ROWS, MXU_TILE))
        acc = alpha * acc_scr[:, cols]
        blocks = []
        for r in range(ck // SOFTMAX_ROWS):
            r0 = r * SOFTMAX_ROWS
            pb = jnp.exp2(s_scr[r0:r0 + SOFTMAX_ROWS, cols] - m_blk)
            blocks.append(pb.astype(BF16))
        p_col = jnp.concatenate(blocks, axis=0)
        acc_scr[:, cols] = acc + jnp.dot(vt_ref[c], p_col, preferred_element_type=F32)

    def step(c, q_following, c_following):
        for nt in range(n_col):
            softmax_pv(c, nt)
            scores(q_following, c_following, nt)

    m_scr[...] = jnp.full(m_scr.shape, -jnp.inf, F32)
    acc_scr[...] = jnp.zeros(acc_scr.shape, F32)

    @pl.when(tile == 0)
    def _():
        for nt in range(n_col):
            scores(qs, 0, nt)

    def one(c, carry):
        step(c, qs, c + 1)
        return carry

    lax.fori_loop(0, n_chunks - 1, one, 0)
    step(n_chunks - 1, qs_next, 0)
    dv = acc_scr.shape[0] - ONES_ROWS
    return acc_scr[0:dv, :], acc_scr[dv:dv + 1, :]


def _flash_scratch(m_rows, ck, dv):
    return [pltpu.VMEM((ck, m_rows), F32), pltpu.VMEM((SUBLANES, m_rows), F32),
            pltpu.VMEM((1, m_rows), F32), pltpu.VMEM((dv + ONES_ROWS, m_rows), F32)]


def _vt_spec(vt, k_row0, n_k, ck):
    _, n_all, rows, width = vt.shape
    if n_k == n_all * width:
        assert k_row0 == 0 and ck == width
        return pl.BlockSpec((None, n_all, rows, width), lambda h, i: (h, 0, 0, 0))
    assert n_k == ck and k_row0 % width % n_k == 0 and (k_row0 % width) + n_k <= width
    chunk, lane_blk = k_row0 // width, (k_row0 % width) // n_k
    return pl.BlockSpec((None, 1, rows, n_k), lambda h, i: (h, chunk, 0, lane_blk))


def _gqa_kernel(q_ref, qn_ref, k_ref, vt_ref, g_ref, o_ref, *scratch, tq, ck, n_chunks):
    hd = C_HEAD_DIM
    stack = lambda ref: jnp.concatenate([ref[:, h * hd:(h + 1) * hd] for h in range(C_GROUP)],
                                        axis=0)
    acc, l = _flash_t(stack(q_ref), stack(qn_ref), k_ref, vt_ref, *scratch, ck, n_chunks,
                      pl.program_id(1))
    o_t = acc / l
    for h in range(C_GROUP):
        gate = g_ref[:, h * hd:(h + 1) * hd].astype(F32)
        o = o_t[:, h * tq:(h + 1) * tq].T
        o_ref[:, h * hd:(h + 1) * hd] = (o * _silu(gate)).astype(BF16)


def _gqa_attention(p, vt, *, q_row0, n_q, k_row0, n_k):
    t = p.shape[0]
    hd = C_HEAD_DIM
    tq = min(ATT_ROWS // C_GROUP, n_q)
    ck = min(KV_CHUNK, n_k)
    n_chunks = n_k // ck
    qb0, kb0 = q_row0 // tq, k_row0 // n_k
    w = C_GROUP * hd
    k_col0 = C_HEADS
    g_col0 = (C_HEADS + 2 * C_KV_HEADS) * hd // w
    kern = functools.partial(_gqa_kernel, tq=tq, ck=ck, n_chunks=n_chunks)
    last = n_q // tq - 1
    in_specs = [
        pl.BlockSpec((tq, w), lambda h, i: (qb0 + i, h)),
        pl.BlockSpec((tq, w), lambda h, i: (qb0 + jnp.minimum(i + 1, last), h)),
        pl.BlockSpec((n_k, hd), lambda h, i: (kb0, k_col0 + h)),
        _vt_spec(vt, k_row0, n_k, ck),
        pl.BlockSpec((tq, w), lambda h, i: (qb0 + i, g_col0 + h)),
    ]
    args = [p, p, p, vt, p]
    return pl.pallas_call(
        kern,
        grid=(C_KV_HEADS, n_q // tq),
        in_specs=in_specs,
        out_specs=pl.BlockSpec((tq, w), lambda h, i: (i, h)),
        out_shape=jax.ShapeDtypeStruct((n_q, D_MODEL), BF16),
        scratch_shapes=_flash_scratch(C_GROUP * tq, ck, hd),
        compiler_params=_cparams(("parallel", "arbitrary")),
        name="gqa_attention",
    )(*args)


def _diff_kernel(lam_ref, sub_ref, q_ref, qn_ref, k_ref, vt_ref, g_ref, o_ref, *scratch, tq,
                 ck, n_chunks, lam_init):
    def stack(ref):
        q = ref[...]
        lane = lax.broadcasted_iota(jnp.int32, q.shape, 1)
        zero = jnp.zeros_like(q)
        first_map = (lane & (A_HEAD_DIM // 2)) == 0
        return jnp.concatenate([jnp.where(first_map, q, zero),
                                jnp.where(first_map, zero, q)], axis=0)

    acc, l = _flash_t(stack(q_ref), stack(qn_ref), k_ref, vt_ref, *scratch, ck, n_chunks,
                      pl.program_id(1))
    o_t = acc / l
    lp = lam_ref[...]
    lam = (jnp.exp(jnp.sum(lp[0:1] * lp[1:2], axis=-1, keepdims=True))
           - jnp.exp(jnp.sum(lp[2:3] * lp[3:4], axis=-1, keepdims=True)) + lam_init)
    od = (o_t[:, :tq] - lam * o_t[:, tq:]).T
    ms = jnp.mean(od * od, axis=-1, keepdims=True)
    y = od * lax.rsqrt(ms + EPS) * sub_ref[...] * (1.0 - lam_init)
    o_ref[...] = (y * _silu(g_ref[...].astype(F32))).astype(BF16)


def _diff_attention(p, vt, lam_p, subln, *, lam_init, q_row0, n_q, k_row0, n_k):
    t = p.shape[0]
    tq = min(ATT_ROWS // 2, n_q)
    ck = min(KV_CHUNK, n_k)
    n_chunks = n_k // ck
    qb0, kb0 = q_row0 // tq, k_row0 // n_k
    w = A_V_DIM
    last = n_q // tq - 1
    kern = functools.partial(_diff_kernel, tq=tq, ck=ck, n_chunks=n_chunks, lam_init=lam_init)
    in_specs = [
        pl.BlockSpec(lam_p.shape, lambda h, i: (0, 0)),
        pl.BlockSpec((1, w), lambda h, i: (0, 0)),
        pl.BlockSpec((tq, w), lambda h, i: (qb0 + i, h)),
        pl.BlockSpec((tq, w), lambda h, i: (qb0 + jnp.minimum(i + 1, last), h)),
        pl.BlockSpec((n_k, w), lambda h, i: (kb0, A_HEADS + h)),
        _vt_spec(vt, k_row0, n_k, ck),
        pl.BlockSpec((tq, w), lambda h, i: (qb0 + i, 3 * A_HEADS + h)),
    ]
    args = [lam_p, subln.reshape(1, w), p, p, p, vt, p]
    return pl.pallas_call(
        kern,
        grid=(A_HEADS, n_q // tq),
        in_specs=in_specs,
        out_specs=pl.BlockSpec((tq, w), lambda h, i: (i, h)),
        out_shape=jax.ShapeDtypeStruct((n_q, A_WIDTH), BF16),
        scratch_shapes=_flash_scratch(2 * tq, ck, w),
        compiler_params=_cparams(("parallel", "arbitrary")),
        name="diff_attention",
    )(*args)


def _ret_kernel(a_ref, qf_ref, kf_ref, vf_ref, qb_ref, kb_ref, vb_ref, of_ref, ob_ref,
                s_ref, dec_ref, qd_ref, kd_ref, cd_ref):
    n = pl.program_id(0)
    c = RET_CHUNK

    @pl.when(n == 0)
    def _():
        i = lax.broadcasted_iota(jnp.int32, (c, c), 0).astype(F32)
        j = lax.broadcasted_iota(jnp.int32, (c, c), 1).astype(F32)
        for d in range(2):
            for h in range(B_HEADS):
                ch = d * B_HEADS + h
                lg = -jnp.exp(a_ref[ch:ch + 1, :])
                rel = (i - j) if d == 0 else (j - i)
                dec_ref[ch] = jnp.where(rel >= 0, jnp.exp(lg * jnp.maximum(rel, 0.0)), 0.0)
                qd_ref[ch] = jnp.exp(lg * ((i + 1.0) if d == 0 else (c - i)))
                kd_ref[ch] = jnp.exp(lg * ((c - 1.0 - i) if d == 0 else i))
                cd_ref[ch] = jnp.exp(lg * float(c)) + jnp.zeros((8, LANES), F32)
                s_ref[ch] = jnp.zeros(s_ref.shape[1:], F32)

    for d, (q_ref, k_ref, v_ref, o_ref) in enumerate(((qf_ref, kf_ref, vf_ref, of_ref),
                                                      (qb_ref, kb_ref, vb_ref, ob_ref))):
        for h in range(B_HEADS):
            ch = d * B_HEADS + h
            q = q_ref[:, h * B_K_DIM:(h + 1) * B_K_DIM]
            k = k_ref[:, h * B_K_DIM:(h + 1) * B_K_DIM]
            v = v_ref[:, h * B_V_DIM:(h + 1) * B_V_DIM]
            st = s_ref[ch]
            sc = lax.dot_general(q, k, (((1,), (1,)), ((), ())),
                                 preferred_element_type=F32) * dec_ref[ch]
            intra = jnp.dot(sc.astype(BF16), v, preferred_element_type=F32)
            qdec = (q.astype(F32) * qd_ref[ch]).astype(BF16)
            cross = jnp.dot(qdec, st.astype(BF16), preferred_element_type=F32)
            o_ref[:, h * B_V_DIM:(h + 1) * B_V_DIM] = intra + cross
            kdec = (k.astype(F32) * kd_ref[ch]).T.astype(BF16)
            upd = jnp.dot(kdec, v, preferred_element_type=F32)
            s_ref[ch] = st * cd_ref[ch][0:1, 0:1] + upd


def _retention(p, ret_decay, *, n_lat):
    t = p.shape[0]
    c = RET_CHUNK
    nc = t // c
    n_lat_c = n_lat // c
    qw, vw = B_HEADS * B_K_DIM, B_HEADS * B_V_DIM
    q_col = 4 * A_WIDTH // qw
    k_col = q_col + 1
    v_col = (4 * A_WIDTH + 2 * qw) // vw
    fwd = lambda n: (n + n_lat_c) % nc
    bwd = lambda n: nc - 1 - n
    return pl.pallas_call(
        _ret_kernel,
        grid=(nc,),
        in_specs=[
            pl.BlockSpec((2 * B_HEADS, 1), lambda n: (0, 0)),
            pl.BlockSpec((c, qw), lambda n: (fwd(n), q_col)),
            pl.BlockSpec((c, qw), lambda n: (fwd(n), k_col)),
            pl.BlockSpec((c, vw), lambda n: (fwd(n), v_col)),
            pl.BlockSpec((c, qw), lambda n: (bwd(n), q_col)),
            pl.BlockSpec((c, qw), lambda n: (bwd(n), k_col)),
            pl.BlockSpec((c, vw), lambda n: (bwd(n), v_col)),
        ],
        out_specs=[
            pl.BlockSpec((c, vw), lambda n: (fwd(n), 0)),
            pl.BlockSpec((c, vw), lambda n: (bwd(n), 0)),
        ],
        out_shape=[jax.ShapeDtypeStruct((t, vw), F32), jax.ShapeDtypeStruct((t, vw), F32)],
        scratch_shapes=[
            pltpu.VMEM((2 * B_HEADS, B_K_DIM, B_V_DIM), F32),
            pltpu.VMEM((2 * B_HEADS, c, c), F32),
            pltpu.VMEM((2 * B_HEADS, c, B_K_DIM), F32),
            pltpu.VMEM((2 * B_HEADS, c, B_K_DIM), F32),
            pltpu.VMEM((2 * B_HEADS, 8, LANES), F32),
        ],
        compiler_params=_cparams(("arbitrary",)),
        name="retention",
    )(ret_decay.reshape(2 * B_HEADS, 1), p, p, p, p, p, p)


def _out_tail(x_ref, y, mod_ref, fin_ref, o_ref, *, n_lat, tm, final):
    d = x_ref.shape[1]
    row = pl.program_id(0) * tm + lax.broadcasted_iota(jnp.int32, (tm, 1), 0)
    gate = jnp.where(row >= n_lat, mod_ref[1:2, 2 * d:3 * d], mod_ref[0:1, 2 * d:3 * d])
    xn = x_ref[...] + gate * y
    if final:
        ms = jnp.mean(xn * xn, axis=-1, keepdims=True)
        xn = xn * lax.rsqrt(ms + EPS) * fin_ref[...]
    o_ref[...] = xn


def _row_kind(lat_ref, ctx_ref, n_lat, tm):
    return jnp.where(pl.program_id(0) * tm >= n_lat, ctx_ref[...], lat_ref[...])


def _out_even_kernel(x_ref, ya_ref, yac_ref, of_ref, ob_ref, bg_ref, gn_ref, w_ref, mod_ref,
                     fin_ref, o_ref, *, n_lat, tm, final):
    ob = of_ref[...] + ob_ref[...]
    gate = _silu(bg_ref[...].astype(F32))
    parts = []
    for h in range(B_HEADS):
        sl = slice(h * B_V_DIM, (h + 1) * B_V_DIM)
        z = ob[:, sl]
        zc = z - jnp.mean(z, axis=-1, keepdims=True)
        yh = zc * lax.rsqrt(jnp.mean(zc * zc, axis=-1, keepdims=True) + EPS) * gn_ref[:, sl]
        parts.append((yh * gate[:, sl]).astype(BF16))
    yb = jnp.concatenate(parts, axis=1)
    ka = ya_ref.shape[1]
    y = (jnp.dot(_row_kind(ya_ref, yac_ref, n_lat, tm), w_ref[0:ka, :],
                 preferred_element_type=F32)
         + jnp.dot(yb, w_ref[ka:, :], preferred_element_type=F32))
    _out_tail(x_ref, y, mod_ref, fin_ref, o_ref, n_lat=n_lat, tm=tm, final=final)


def _out_odd_kernel(x_ref, y_ref, yc_ref, w_ref, mod_ref, fin_ref, o_ref, *, n_lat, tm, final):
    y = jnp.dot(_row_kind(y_ref, yc_ref, n_lat, tm), w_ref[...], preferred_element_type=F32)
    _out_tail(x_ref, y, mod_ref, fin_ref, o_ref, n_lat=n_lat, tm=tm, final=final)


def _out_proj(xs, mixer_args, mixer_specs, w_out, layer, mod_l, final_norm, *, n_lat, n_rows,
              even, final):
    t, d = xs.shape
    tm = OUT_TM
    body = _out_even_kernel if even else _out_odd_kernel
    kern = functools.partial(body, n_lat=n_lat, tm=tm, final=final)
    row_spec = pl.BlockSpec((tm, d), lambda i: (i, 0))
    const = lambda shape: pl.BlockSpec(shape, lambda i: (0, 0))
    return pl.pallas_call(
        kern,
        grid=(n_rows // tm,),
        in_specs=[row_spec] + mixer_specs + [
            pl.BlockSpec((None,) + w_out.shape[1:], lambda i: (layer, 0, 0)),
            const(mod_l.shape), const((1, d))],
        out_specs=row_spec,
        out_shape=jax.ShapeDtypeStruct((n_rows, d), F32),
        compiler_params=_cparams(("parallel",)),
        name="out_proj",
    )(xs, *mixer_args, w_out, mod_l, final_norm.reshape(1, d))


def _pair_layout(w, n_cols, head_dim):
    lead = w.shape[:-1]
    r = w[..., :n_cols].reshape(*lead, n_cols // LANES, LANES // head_dim, 2, 2, head_dim // 4)
    r = jnp.moveaxis(r, -2, -4)
    return jnp.concatenate([r.reshape(*lead, n_cols), w[..., n_cols:]], axis=-1)


def _rope_tables(n_lat, head_dim):
    n_rows = n_lat // GRID_W
    axis_dim = head_dim // 2
    inv = ROPE_THETA ** (-jnp.arange(0, axis_dim, 2, dtype=F32) / axis_dim)
    ar = jnp.arange(n_rows, dtype=jnp.int32).astype(F32)[:, None] * inv
    ac = jnp.arange(GRID_W, dtype=jnp.int32).astype(F32)[:, None] * inv
    per_row = lambda a: jnp.broadcast_to(a[:, None, :], (n_rows, GRID_W, a.shape[-1]))
    per_col = lambda a: jnp.broadcast_to(a[None, :, :], (n_rows, GRID_W, a.shape[-1]))
    heads = LANES // head_dim
    half = lambda r, c: jnp.concatenate([per_row(r), per_col(c)] * heads, axis=-1)
    cos_h, sin_h = half(jnp.cos(ar), jnp.cos(ac)), half(jnp.sin(ar), jnp.sin(ac))
    cos = jnp.concatenate([cos_h, cos_h], axis=-1).reshape(n_lat, LANES)
    sin = jnp.concatenate([-sin_h, sin_h], axis=-1).reshape(n_lat, LANES)
    cos = jnp.concatenate([cos, jnp.ones((CTX_LEN, LANES), F32)], axis=0)
    sin = jnp.concatenate([sin, jnp.zeros((CTX_LEN, LANES), F32)], axis=0)
    return cos, sin


def kernel(x, c, ctx, c_ctx, norm_g, w_mod, b_mod, ev_w_in, ev_w_out, diff_lambda, diff_subln,
           ret_decay, ret_gn, od_w_in, od_w_out, qk_norm, final_norm):
    assert x.shape[0] == 1 and ctx.shape[1] == CTX_LEN and x.shape[2] == D_MODEL
    n_lat = x.shape[1]
    t = n_lat + CTX_LEN
    d = D_MODEL
    assert t % PROJ_TM == 0 and KV_CHUNK == PROJ_TM and n_lat % (ATT_ROWS // 2) == 0
    assert OUT_TM == CTX_LEN and n_lat % OUT_TM == 0

    xs = jnp.concatenate([x[0], ctx[0]], axis=0)
    s_in = jnp.zeros((8, d), F32).at[0].set(c[0]).at[1].set(c_ctx)
    mod = _modulation(s_in, w_mod, b_mod)
    rope_a = _rope_tables(n_lat, A_HEAD_DIM)
    rope_c = _rope_tables(n_lat, C_HEAD_DIM)
    ev_w_in_b = _pair_layout(ev_w_in, 2 * A_QK, A_HEAD_DIM).astype(BF16)
    od_w_in_b = _pair_layout(od_w_in, (C_HEADS + C_KV_HEADS) * C_HEAD_DIM, C_HEAD_DIM).astype(BF16)
    ev_w_out_b, od_w_out_b = ev_w_out.astype(BF16), od_w_out.astype(BF16)
    qk_norm = _pair_layout(qk_norm, C_HEAD_DIM, C_HEAD_DIM)

    for i in range(DEPTH):
        final = i == DEPTH - 1
        n_rows = n_lat if final else t
        tm = OUT_TM
        lat_rows = lambda r: (jnp.minimum(r, n_lat // tm - 1), 0)
        first = lambda r: (0, 0)
        if i % 2 == 0:
            e = i // 2
            lam_init = 0.8 - 0.6 * math.exp(-0.3 * i)
            p, vt = _proj(xs, mod[i], norm_g[i], ev_w_in_b, e, *rope_a, qk_norm[0], n_lat=n_lat,
                          even=True)
            ya = _diff_attention(p, vt, diff_lambda[e], diff_subln[e], lam_init=lam_init,
                                 q_row0=0, n_q=n_lat, k_row0=0, n_k=t)
            ya_ctx = ya if final else _diff_attention(
                p, vt, diff_lambda[e], diff_subln[e], lam_init=lam_init, q_row0=n_lat,
                n_q=CTX_LEN, k_row0=n_lat, n_k=CTX_LEN)
            o_f, o_b = _retention(p, ret_decay[e], n_lat=n_lat)
            bw = B_WIDTH
            args = [ya, ya_ctx, o_f, o_b, p, ret_gn[e].reshape(1, bw)]
            specs = [pl.BlockSpec((tm, A_WIDTH), lat_rows), pl.BlockSpec((tm, A_WIDTH), first),
                     pl.BlockSpec((tm, bw), lambda r: (r, 0)),
                     pl.BlockSpec((tm, bw), lambda r: (r, 0)),
                     pl.BlockSpec((tm, bw), lambda r: (r, (EV_IN - bw) // bw)),
                     pl.BlockSpec((1, bw), lambda r: (0, 0))]
            xs = _out_proj(xs, args, specs, ev_w_out_b, e, mod[i], final_norm, n_lat=n_lat,
                           n_rows=n_rows, even=True, final=final)
        else:
            o = i // 2
            p, vt = _proj(xs, mod[i], norm_g[i], od_w_in_b, o, *rope_c, qk_norm[o], n_lat=n_lat,
                          even=False)
            y = _gqa_attention(p, vt, q_row0=0, n_q=n_lat, k_row0=0, n_k=t)
            y_ctx = y if final else _gqa_attention(p, vt, q_row0=n_lat, n_q=CTX_LEN,
                                                   k_row0=n_lat, n_k=CTX_LEN)
            specs = [pl.BlockSpec((tm, d), lat_rows), pl.BlockSpec((tm, d), first)]
            xs = _out_proj(xs, [y, y_ctx], specs, od_w_out_b, o, mod[i], final_norm, n_lat=n_lat,
                           n_rows=n_rows, even=False, final=final)
    return xs[None]
```

```python
import functools
import math

import jax
import jax.numpy as jnp
from jax import lax
from jax.experimental import pallas as pl
from jax.experimental.pallas import tpu as pltpu

F32 = jnp.float32
BF16 = jnp.bfloat16

D_MODEL = 2048
DEPTH = 4
GRID_W = 64
CTX_LEN = 256
RET_CHUNK = 128
ROPE_THETA = 10000.0
EPS = 1e-6

A_WIDTH = D_MODEL // 2
A_HEADS = 8
A_HEAD_DIM = A_WIDTH // A_HEADS // 2
A_V_DIM = 2 * A_HEAD_DIM
B_WIDTH = D_MODEL - A_WIDTH
B_HEADS = 4
B_V_DIM = B_WIDTH // B_HEADS
B_K_DIM = B_V_DIM // 2
A_QK = 2 * A_HEADS * A_HEAD_DIM
EV_IN = 7168

C_HEADS = 16
C_KV_HEADS = 4
C_GROUP = C_HEADS // C_KV_HEADS
C_HEAD_DIM = D_MODEL // C_HEADS
OD_IN = 5120

LANES = 128
SUBLANES = 8
MXU_TILE = 256
ONES_ROWS = 16
SCORE_ROWS = 256
SOFTMAX_ROWS = 32
MAX_CHAINS = 4
LOG2E = math.log2(math.e)
VMEM_LIMIT_BYTES = 56 * 1024 * 1024
PROJ_TM = 1280
PROJ_TN = 512
NORM_ROWS = 64
NORM_SUB = 16
RET_STEP = 2
OUT_TM = 256
KV_CHUNK = 1280
ATT_ROWS = 4096


def _silu(x):
    return x / (1.0 + jnp.exp(-x))


def _cparams(sem):
    return pltpu.CompilerParams(dimension_semantics=sem, vmem_limit_bytes=VMEM_LIMIT_BYTES)


def _mod_kernel(s_ref, w_ref, b_ref, o_ref):
    s = _silu(s_ref[...])
    acc = jnp.dot(s.astype(BF16), w_ref[0].astype(BF16), preferred_element_type=F32)
    o_ref[0] = acc + b_ref[0]


def _modulation(s_in, w_mod, b_mod):
    depth, d, n = w_mod.shape
    tn = 768
    return pl.pallas_call(
        _mod_kernel,
        grid=(depth, n // tn),
        in_specs=[
            pl.BlockSpec((8, d), lambda l, j: (0, 0)),
            pl.BlockSpec((1, d, tn), lambda l, j: (l, 0, j)),
            pl.BlockSpec((1, 1, tn), lambda l, j: (l, 0, j)),
        ],
        out_specs=pl.BlockSpec((1, 8, tn), lambda l, j: (l, 0, j)),
        out_shape=jax.ShapeDtypeStruct((depth, 8, n), F32),
        compiler_params=_cparams(("parallel", "parallel")),
        name="modulation",
    )(s_in, w_mod, b_mod.reshape(depth, 1, n))


def _rope(x, cos, sin, quarter):
    lane = lax.broadcasted_iota(jnp.int32, x.shape, 1)
    first = (lane & quarter) == 0
    partner = jnp.where(first, pltpu.roll(x, LANES - quarter, 1), pltpu.roll(x, quarter, 1))
    return x * cos + partner * sin


def _proj_kernel(x_ref, mod_ref, g_ref, w_ref, cos_ref, sin_ref, qk_ref, o_ref, vt_ref, h_ref, *,
                 n_lat, tm, tn, even, v_tile0, n_v_tiles):
    i = pl.program_id(0)
    j = pl.program_id(1)
    d = x_ref.shape[1]

    @pl.when(j == 0)
    def _():
        gain = [g_ref[...] * (1.0 + mod_ref[k:k + 1, d:2 * d]) for k in range(2)]
        shift = [mod_ref[k:k + 1, 0:d] for k in range(2)]

        def norm_rows(r, carry):
            r0 = pl.multiple_of(r * NORM_ROWS, NORM_ROWS)
            is_ctx = i * tm + r0 >= n_lat
            gm = jnp.where(is_ctx, gain[1], gain[0])
            sh = jnp.where(is_ctx, shift[1], shift[0])
            for u in range(NORM_ROWS // NORM_SUB):
                rows = pl.ds(r0 + u * NORM_SUB, NORM_SUB)
                x = x_ref[rows, :]
                ms = jnp.mean(x * x, axis=-1, keepdims=True)
                h_ref[rows, :] = (x * lax.rsqrt(ms + EPS) * gm + sh).astype(BF16)
            return carry

        lax.fori_loop(0, tm // NORM_ROWS, norm_rows, 0)

    acc = jnp.dot(h_ref[...], w_ref[...], preferred_element_type=F32)
    groups = tn // LANES

    @pl.when(jnp.logical_and(j >= v_tile0, j < v_tile0 + n_v_tiles))
    def _():
        row = lax.broadcasted_iota(jnp.int32, (ONES_ROWS, tm), 0)
        ones = jnp.where(row == 0, 1.0, 0.0).astype(BF16)
        for g in range(groups):
            vt_ref[g, 0:LANES, :] = acc[:, g * LANES:(g + 1) * LANES].T.astype(BF16)
            vt_ref[g, LANES:LANES + ONES_ROWS, :] = ones

    if even:
        n_rope = 2 * (2 * A_HEADS * A_HEAD_DIM) // tn
        n_q = n_rope // 2
        bk_tile = (4 * A_WIDTH + B_HEADS * B_K_DIM) // tn

        @pl.when(j < n_rope)
        def _():
            cos = cos_ref[...]
            sin = sin_ref[...]
            qscale = jnp.where(j < n_q, A_HEAD_DIM ** -0.5 * LOG2E, 1.0).astype(F32)
            for g in range(groups):
                sl = slice(g * LANES, (g + 1) * LANES)
                o_ref[:, sl] = (_rope(acc[:, sl], cos, sin, A_HEAD_DIM // 4) * qscale).astype(BF16)

        @pl.when(j == bk_tile)
        def _():
            o_ref[...] = (acc * (B_K_DIM ** -0.5)).astype(BF16)

        @pl.when(jnp.logical_and(j >= n_rope, j != bk_tile))
        def _():
            o_ref[...] = acc.astype(BF16)
    else:
        n_q = (C_HEADS * C_HEAD_DIM) // tn
        n_qk = n_q + (C_KV_HEADS * C_HEAD_DIM) // tn

        @pl.when(j < n_qk)
        def _():
            cos = cos_ref[...]
            sin = sin_ref[...]
            is_q = j < n_q
            gain = jnp.where(is_q, qk_ref[0:1, :], qk_ref[1:2, :])
            qscale = jnp.where(is_q, C_HEAD_DIM ** -0.5 * LOG2E, 1.0).astype(F32)
            for g in range(groups):
                sl = slice(g * LANES, (g + 1) * LANES)
                xg = acc[:, sl]
                ms = jnp.mean(xg * xg, axis=-1, keepdims=True)
                yg = xg * lax.rsqrt(ms + EPS) * gain
                o_ref[:, sl] = (_rope(yg, cos, sin, C_HEAD_DIM // 4) * qscale).astype(BF16)

        @pl.when(j >= n_qk)
        def _():
            o_ref[...] = acc.astype(BF16)


def _proj(xs, mod_l, norm_g, w_in, layer, cos, sin, qk_g, *, n_lat, even):
    t, d = xs.shape
    n = w_in.shape[2]
    tm, tn = PROJ_TM, PROJ_TN
    hpt = tn // LANES
    v_col0, v_heads = (2 * A_QK, A_HEADS) if even else (C_HEADS * C_HEAD_DIM + C_KV_HEADS * C_HEAD_DIM,
                                                        C_KV_HEADS)
    v_tile0, n_v_tiles = v_col0 // tn, v_heads // hpt
    kern = functools.partial(_proj_kernel, n_lat=n_lat, tm=tm, tn=tn, even=even, v_tile0=v_tile0,
                             n_v_tiles=n_v_tiles)
    vt_map = lambda i, j: (jnp.clip(j - v_tile0, 0, n_v_tiles - 1), i, 0, 0)
    return pl.pallas_call(
        kern,
        grid=(t // tm, n // tn),
        in_specs=[
            pl.BlockSpec((tm, d), lambda i, j: (i, 0)),
            pl.BlockSpec(mod_l.shape, lambda i, j: (0, 0)),
            pl.BlockSpec((1, d), lambda i, j: (0, 0)),
            pl.BlockSpec((None, d, tn), lambda i, j: (layer, 0, j)),
            pl.BlockSpec((tm, LANES), lambda i, j: (i, 0)),
            pl.BlockSpec((tm, LANES), lambda i, j: (i, 0)),
            pl.BlockSpec(qk_g.shape, lambda i, j: (0, 0)),
        ],
        out_specs=[pl.BlockSpec((tm, tn), lambda i, j: (i, j)),
                   pl.BlockSpec((hpt, None, LANES + ONES_ROWS, tm), vt_map)],
        out_shape=[jax.ShapeDtypeStruct((t, n), BF16),
                   jax.ShapeDtypeStruct((v_heads, t // tm, LANES + ONES_ROWS, tm), BF16)],
        scratch_shapes=[pltpu.VMEM((tm, d), BF16)],
        compiler_params=_cparams(("parallel", "arbitrary")),
        name="proj_even" if even else "proj_odd",
    )(xs, mod_l, norm_g.reshape(1, d), w_in, cos, sin, qk_g)


def _flash_t(qs, qs_next, k_ref, vt_ref, s_scr, mx_scr, m_scr, acc_scr, ck, n_chunks, tile):
    sub = SUBLANES
    n_col = qs.shape[0] // MXU_TILE
    score_rows = min(SCORE_ROWS, ck)

    def scores(q, c, nt):
        cols = slice(nt * MXU_TILE, (nt + 1) * MXU_TILE)
        parts = None
        for kb in range(ck // score_rows):
            rows = pl.ds(pl.multiple_of(c * ck + kb * score_rows, score_rows), score_rows)
            s = lax.dot_general(k_ref[rows, :], q[cols], (((1,), (1,)), ((), ())),
                                preferred_element_type=F32)
            s_scr[kb * score_rows:(kb + 1) * score_rows, cols] = s
            blk = [s[t * sub:(t + 1) * sub] for t in range(score_rows // sub)]
            if parts is None:
                parts, blk = blk[:MAX_CHAINS], blk[MAX_CHAINS:]
            for r, b in enumerate(blk):
                parts[r % MAX_CHAINS] = jnp.maximum(parts[r % MAX_CHAINS], b)
        mx_scr[:, cols] = functools.reduce(jnp.maximum, parts)

    def softmax_pv(c, nt):
        cols = slice(nt * MXU_TILE, (nt + 1) * MXU_TILE)
        m_prev = m_scr[:, cols]
        m_new = jnp.maximum(m_prev, jnp.max(mx_scr[:, cols], axis=0, keepdims=True))
        alpha = jnp.exp2(m_prev - m_new)
        m_scr[:, cols] = m_new
        m_blk = jnp.broadcast_to(m_new, (SOFTMAX_ROWS, MXU_TILE))
        acc = alpha * acc_scr[:, cols]
        blocks = []
        for r in range(ck // SOFTMAX_ROWS):
            r0 = r * SOFTMAX_ROWS
            pb = jnp.exp2(s_scr[r0:r0 + SOFTMAX_ROWS, cols] - m_blk)
            blocks.append(pb.astype(BF16))
        p_col = jnp.concatenate(blocks, axis=0)
        acc_scr[:, cols] = acc + jnp.dot(vt_ref[c], p_col, preferred_element_type=F32)

    def step(c, q_following, c_following):
        for nt in range(n_col):
            softmax_pv(c, nt)
            scores(q_following, c_following, nt)

    m_scr[...] = jnp.full(m_scr.shape, -jnp.inf, F32)
    acc_scr[...] = jnp.zeros(acc_scr.shape, F32)

    @pl.when(tile == 0)
    def _():
        for nt in range(n_col):
            scores(qs, 0, nt)

    def one(c, carry):
        step(c, qs, c + 1)
        return carry

    lax.fori_loop(0, n_chunks - 1, one, 0)
    step(n_chunks - 1, qs_next, 0)
    dv = acc_scr.shape[0] - ONES_ROWS
    return acc_scr[0:dv, :], acc_scr[dv:dv + 1, :]


def _flash_scratch(m_rows, ck, dv):
    return [pltpu.VMEM((ck, m_rows), F32), pltpu.VMEM((SUBLANES, m_rows), F32),
            pltpu.VMEM((1, m_rows), F32), pltpu.VMEM((dv + ONES_ROWS, m_rows), F32)]


def _vt_spec(vt, k_row0, n_k, ck):
    _, n_all, rows, width = vt.shape
    if n_k == n_all * width:
        assert k_row0 == 0 and ck == width
        return pl.BlockSpec((None, n_all, rows, width), lambda h, i: (h, 0, 0, 0))
    assert n_k == ck and k_row0 % width % n_k == 0 and (k_row0 % width) + n_k <= width
    chunk, lane_blk = k_row0 // width, (k_row0 % width) // n_k
    return pl.BlockSpec((None, 1, rows, n_k), lambda h, i: (h, chunk, 0, lane_blk))


def _gqa_kernel(q_ref, qn_ref, k_ref, vt_ref, g_ref, o_ref, *scratch, tq, ck, n_chunks):
    hd = C_HEAD_DIM
    stack = lambda ref: jnp.concatenate([ref[:, h * hd:(h + 1) * hd] for h in range(C_GROUP)],
                                        axis=0)
    acc, l = _flash_t(stack(q_ref), stack(qn_ref), k_ref, vt_ref, *scratch, ck, n_chunks,
                      pl.program_id(1))
    o_t = acc / l
    for h in range(C_GROUP):
        gate = g_ref[:, h * hd:(h + 1) * hd].astype(F32)
        o = o_t[:, h * tq:(h + 1) * tq].T
        o_ref[:, h * hd:(h + 1) * hd] = (o * _silu(gate)).astype(BF16)


def _gqa_attention(p, vt, *, q_row0, n_q, k_row0, n_k):
    t = p.shape[0]
    hd = C_HEAD_DIM
    tq = min(ATT_ROWS // C_GROUP, n_q)
    ck = min(KV_CHUNK, n_k)
    n_chunks = n_k // ck
    qb0, kb0 = q_row0 // tq, k_row0 // n_k
    w = C_GROUP * hd
    k_col0 = C_HEADS
    g_col0 = (C_HEADS + 2 * C_KV_HEADS) * hd // w
    kern = functools.partial(_gqa_kernel, tq=tq, ck=ck, n_chunks=n_chunks)
    last = n_q // tq - 1
    in_specs = [
        pl.BlockSpec((tq, w), lambda h, i: (qb0 + i, h)),
        pl.BlockSpec((tq, w), lambda h, i: (qb0 + jnp.minimum(i + 1, last), h)),
        pl.BlockSpec((n_k, hd), lambda h, i: (kb0, k_col0 + h)),
        _vt_spec(vt, k_row0, n_k, ck),
        pl.BlockSpec((tq, w), lambda h, i: (qb0 + i, g_col0 + h)),
    ]
    args = [p, p, p, vt, p]
    return pl.pallas_call(
        kern,
        grid=(C_KV_HEADS, n_q // tq),
        in_specs=in_specs,
        out_specs=pl.BlockSpec((tq, w), lambda h, i: (i, h)),
        out_shape=jax.ShapeDtypeStruct((n_q, D_MODEL), BF16),
        scratch_shapes=_flash_scratch(C_GROUP * tq, ck, hd),
        compiler_params=_cparams(("parallel", "arbitrary")),
        name="gqa_attention",
    )(*args)


def _diff_kernel(lam_ref, sub_ref, q_ref, qn_ref, k_ref, vt_ref, g_ref, o_ref, *scratch, tq,
                 ck, n_chunks, lam_init):
    def stack(ref):
        q = ref[...]
        lane = lax.broadcasted_iota(jnp.int32, q.shape, 1)
        zero = jnp.zeros_like(q)
        return jnp.concatenate([jnp.where(lane < A_HEAD_DIM, q, zero),
                                jnp.where(lane >= A_HEAD_DIM, q, zero)], axis=0)

    acc, l = _flash_t(stack(q_ref), stack(qn_ref), k_ref, vt_ref, *scratch, ck, n_chunks,
                      pl.program_id(1))
    o_t = acc / l
    lp = lam_ref[...]
    lam = (jnp.exp(jnp.sum(lp[0:1] * lp[1:2], axis=-1, keepdims=True))
           - jnp.exp(jnp.sum(lp[2:3] * lp[3:4], axis=-1, keepdims=True)) + lam_init)
    od = (o_t[:, :tq] - lam * o_t[:, tq:]).T
    ms = jnp.mean(od * od, axis=-1, keepdims=True)
    y = od * lax.rsqrt(ms + EPS) * sub_ref[...] * (1.0 - lam_init)
    o_ref[...] = (y * _silu(g_ref[...].astype(F32))).astype(BF16)


def _diff_attention(p, vt, lam_p, subln, *, lam_init, q_row0, n_q, k_row0, n_k):
    t = p.shape[0]
    tq = min(ATT_ROWS // 2, n_q)
    ck = min(KV_CHUNK, n_k)
    n_chunks = n_k // ck
    qb0, kb0 = q_row0 // tq, k_row0 // n_k
    w = A_V_DIM
    last = n_q // tq - 1
    kern = functools.partial(_diff_kernel, tq=tq, ck=ck, n_chunks=n_chunks, lam_init=lam_init)
    in_specs = [
        pl.BlockSpec(lam_p.shape, lambda h, i: (0, 0)),
        pl.BlockSpec((1, w), lambda h, i: (0, 0)),
        pl.BlockSpec((tq, w), lambda h, i: (qb0 + i, h)),
        pl.BlockSpec((tq, w), lambda h, i: (qb0 + jnp.minimum(i + 1, last), h)),
        pl.BlockSpec((n_k, w), lambda h, i: (kb0, A_HEADS + h)),
        _vt_spec(vt, k_row0, n_k, ck),
        pl.BlockSpec((tq, w), lambda h, i: (qb0 + i, 3 * A_HEADS + h)),
    ]
    args = [lam_p, subln.reshape(1, w), p, p, p, vt, p]
    return pl.pallas_call(
        kern,
        grid=(A_HEADS, n_q // tq),
        in_specs=in_specs,
        out_specs=pl.BlockSpec((tq, w), lambda h, i: (i, h)),
        out_shape=jax.ShapeDtypeStruct((n_q, A_WIDTH), BF16),
        scratch_shapes=_flash_scratch(2 * tq, ck, w),
        compiler_params=_cparams(("parallel", "arbitrary")),
        name="diff_attention",
    )(*args)


def _ret_kernel(a_ref, qf_ref, kf_ref, vf_ref, qb_ref, kb_ref, vb_ref, of_ref, ob_ref,
                s_ref, dec_ref, qd_ref, kd_ref, cd_ref):
    n = pl.program_id(0)
    c = RET_CHUNK

    @pl.when(n == 0)
    def _():
        i = lax.broadcasted_iota(jnp.int32, (c, c), 0).astype(F32)
        j = lax.broadcasted_iota(jnp.int32, (c, c), 1).astype(F32)
        for d in range(2):
            for h in range(B_HEADS):
                ch = d * B_HEADS + h
                lg = -jnp.exp(a_ref[ch:ch + 1, :])
                rel = (i - j) if d == 0 else (j - i)
                dec_ref[ch] = jnp.where(rel >= 0, jnp.exp(lg * jnp.maximum(rel, 0.0)), 0.0)
                qd_ref[ch] = jnp.exp(lg * ((i + 1.0) if d == 0 else (c - i)))
                kd_ref[ch] = jnp.exp(lg * ((c - 1.0 - i) if d == 0 else i))
                cd_ref[ch] = jnp.exp(lg * float(c)) + jnp.zeros((8, LANES), F32)
                s_ref[ch] = jnp.zeros(s_ref.shape[1:], F32)

    for u in range(RET_STEP):
        for d, (q_ref, k_ref, v_ref, o_ref) in enumerate(((qf_ref, kf_ref, vf_ref, of_ref),
                                                          (qb_ref, kb_ref, vb_ref, ob_ref))):
            sub = u if d == 0 else RET_STEP - 1 - u
            rows = slice(sub * c, (sub + 1) * c)
            for h in range(B_HEADS):
                ch = d * B_HEADS + h
                q = q_ref[rows, h * B_K_DIM:(h + 1) * B_K_DIM]
                k = k_ref[rows, h * B_K_DIM:(h + 1) * B_K_DIM]
                v = v_ref[rows, h * B_V_DIM:(h + 1) * B_V_DIM]
                st = s_ref[ch]
                sc = lax.dot_general(q, k, (((1,), (1,)), ((), ())),
                                     preferred_element_type=F32) * dec_ref[ch]
                intra = jnp.dot(sc.astype(BF16), v, preferred_element_type=F32)
                qdec = (q.astype(F32) * qd_ref[ch]).astype(BF16)
                cross = jnp.dot(qdec, st.astype(BF16), preferred_element_type=F32)
                o_ref[rows, h * B_V_DIM:(h + 1) * B_V_DIM] = intra + cross
                kdec = (k.astype(F32) * kd_ref[ch]).T.astype(BF16)
                upd = jnp.dot(kdec, v, preferred_element_type=F32)
                s_ref[ch] = st * cd_ref[ch][0:1, 0:1] + upd


def _retention(p, ret_decay, *, n_lat):
    t = p.shape[0]
    c = RET_CHUNK
    assert t % (RET_STEP * c) == 0 and n_lat % (RET_STEP * c) == 0
    rb = RET_STEP * c
    nc = t // rb
    n_lat_c = n_lat // rb
    qw, vw = B_HEADS * B_K_DIM, B_HEADS * B_V_DIM
    q_col = 4 * A_WIDTH // qw
    k_col = q_col + 1
    v_col = (4 * A_WIDTH + 2 * qw) // vw
    fwd = lambda n: (n + n_lat_c) % nc
    bwd = lambda n: nc - 1 - n
    return pl.pallas_call(
        _ret_kernel,
        grid=(nc,),
        in_specs=[
            pl.BlockSpec((2 * B_HEADS, 1), lambda n: (0, 0)),
            pl.BlockSpec((rb, qw), lambda n: (fwd(n), q_col)),
            pl.BlockSpec((rb, qw), lambda n: (fwd(n), k_col)),
            pl.BlockSpec((rb, vw), lambda n: (fwd(n), v_col)),
            pl.BlockSpec((rb, qw), lambda n: (bwd(n), q_col)),
            pl.BlockSpec((rb, qw), lambda n: (bwd(n), k_col)),
            pl.BlockSpec((rb, vw), lambda n: (bwd(n), v_col)),
        ],
        out_specs=[
            pl.BlockSpec((rb, vw), lambda n: (fwd(n), 0)),
            pl.BlockSpec((rb, vw), lambda n: (bwd(n), 0)),
        ],
        out_shape=[jax.ShapeDtypeStruct((t, vw), F32), jax.ShapeDtypeStruct((t, vw), F32)],
        scratch_shapes=[
            pltpu.VMEM((2 * B_HEADS, B_K_DIM, B_V_DIM), F32),
            pltpu.VMEM((2 * B_HEADS, c, c), F32),
            pltpu.VMEM((2 * B_HEADS, c, B_K_DIM), F32),
            pltpu.VMEM((2 * B_HEADS, c, B_K_DIM), F32),
            pltpu.VMEM((2 * B_HEADS, 8, LANES), F32),
        ],
        compiler_params=_cparams(("arbitrary",)),
        name="retention",
    )(ret_decay.reshape(2 * B_HEADS, 1), p, p, p, p, p, p)


def _out_tail(x_ref, y, mod_ref, fin_ref, o_ref, *, n_lat, tm, final):
    d = x_ref.shape[1]
    row = pl.program_id(0) * tm + lax.broadcasted_iota(jnp.int32, (tm, 1), 0)
    gate = jnp.where(row >= n_lat, mod_ref[1:2, 2 * d:3 * d], mod_ref[0:1, 2 * d:3 * d])
    xn = x_ref[...] + gate * y
    if final:
        ms = jnp.mean(xn * xn, axis=-1, keepdims=True)
        xn = xn * lax.rsqrt(ms + EPS) * fin_ref[...]
    o_ref[...] = xn


def _row_kind(lat_ref, ctx_ref, n_lat, tm):
    return jnp.where(pl.program_id(0) * tm >= n_lat, ctx_ref[...], lat_ref[...])


def _out_even_kernel(x_ref, ya_ref, yac_ref, of_ref, ob_ref, bg_ref, gn_ref, w_ref, mod_ref,
                     fin_ref, o_ref, *, n_lat, tm, final):
    ob = of_ref[...] + ob_ref[...]
    gate = _silu(bg_ref[...].astype(F32))
    parts = []
    for h in range(B_HEADS):
        sl = slice(h * B_V_DIM, (h + 1) * B_V_DIM)
        z = ob[:, sl]
        zc = z - jnp.mean(z, axis=-1, keepdims=True)
        yh = zc * lax.rsqrt(jnp.mean(zc * zc, axis=-1, keepdims=True) + EPS) * gn_ref[:, sl]
        parts.append((yh * gate[:, sl]).astype(BF16))
    yb = jnp.concatenate(parts, axis=1)
    ka = ya_ref.shape[1]
    y = (jnp.dot(_row_kind(ya_ref, yac_ref, n_lat, tm), w_ref[0:ka, :],
                 preferred_element_type=F32)
         + jnp.dot(yb, w_ref[ka:, :], preferred_element_type=F32))
    _out_tail(x_ref, y, mod_ref, fin_ref, o_ref, n_lat=n_lat, tm=tm, final=final)


def _out_odd_kernel(x_ref, y_ref, yc_ref, w_ref, mod_ref, fin_ref, o_ref, *, n_lat, tm, final):
    y = jnp.dot(_row_kind(y_ref, yc_ref, n_lat, tm), w_ref[...], preferred_element_type=F32)
    _out_tail(x_ref, y, mod_ref, fin_ref, o_ref, n_lat=n_lat, tm=tm, final=final)


def _out_proj(xs, mixer_args, mixer_specs, w_out, layer, mod_l, final_norm, *, n_lat, n_rows,
              even, final):
    t, d = xs.shape
    tm = OUT_TM
    body = _out_even_kernel if even else _out_odd_kernel
    kern = functools.partial(body, n_lat=n_lat, tm=tm, final=final)
    row_spec = pl.BlockSpec((tm, d), lambda i: (i, 0))
    const = lambda shape: pl.BlockSpec(shape, lambda i: (0, 0))
    return pl.pallas_call(
        kern,
        grid=(n_rows // tm,),
        in_specs=[row_spec] + mixer_specs + [
            pl.BlockSpec((None,) + w_out.shape[1:], lambda i: (layer, 0, 0)),
            const(mod_l.shape), const((1, d))],
        out_specs=row_spec,
        out_shape=jax.ShapeDtypeStruct((n_rows, d), F32),
        compiler_params=_cparams(("parallel",)),
        name="out_proj",
    )(xs, *mixer_args, w_out, mod_l, final_norm.reshape(1, d))


def _rope_tables(n_lat, head_dim):
    n_rows = n_lat // GRID_W
    axis_dim = head_dim // 2
    inv = ROPE_THETA ** (-jnp.arange(0, axis_dim, 2, dtype=F32) / axis_dim)
    ar = jnp.arange(n_rows, dtype=jnp.int32).astype(F32)[:, None] * inv
    ac = jnp.arange(GRID_W, dtype=jnp.int32).astype(F32)[:, None] * inv
    per_row = lambda a: jnp.broadcast_to(a[:, None, :], (n_rows, GRID_W, a.shape[-1]))
    per_col = lambda a: jnp.broadcast_to(a[None, :, :], (n_rows, GRID_W, a.shape[-1]))
    cos_r, sin_r, cos_c, sin_c = jnp.cos(ar), jnp.sin(ar), jnp.cos(ac), jnp.sin(ac)
    cos = jnp.concatenate([per_row(cos_r), per_row(cos_r), per_col(cos_c), per_col(cos_c)],
                          axis=-1).reshape(n_lat, head_dim)
    sin = jnp.concatenate([per_row(-sin_r), per_row(sin_r), per_col(-sin_c), per_col(sin_c)],
                          axis=-1).reshape(n_lat, head_dim)
    reps = LANES // head_dim
    cos, sin = jnp.tile(cos, (1, reps)), jnp.tile(sin, (1, reps))
    cos = jnp.concatenate([cos, jnp.ones((CTX_LEN, LANES), F32)], axis=0)
    sin = jnp.concatenate([sin, jnp.zeros((CTX_LEN, LANES), F32)], axis=0)
    return cos, sin


def kernel(x, c, ctx, c_ctx, norm_g, w_mod, b_mod, ev_w_in, ev_w_out, diff_lambda, diff_subln,
           ret_decay, ret_gn, od_w_in, od_w_out, qk_norm, final_norm):
    assert x.shape[0] == 1 and ctx.shape[1] == CTX_LEN and x.shape[2] == D_MODEL
    n_lat = x.shape[1]
    t = n_lat + CTX_LEN
    d = D_MODEL
    assert t % PROJ_TM == 0 and KV_CHUNK == PROJ_TM and n_lat % (ATT_ROWS // 2) == 0
    assert OUT_TM == CTX_LEN and n_lat % OUT_TM == 0

    xs = jnp.concatenate([x[0], ctx[0]], axis=0)
    s_in = jnp.zeros((8, d), F32).at[0].set(c[0]).at[1].set(c_ctx)
    mod = _modulation(s_in, w_mod, b_mod)
    rope_a = _rope_tables(n_lat, A_HEAD_DIM)
    rope_c = _rope_tables(n_lat, C_HEAD_DIM)
    ev_w_in_b, ev_w_out_b = ev_w_in.astype(BF16), ev_w_out.astype(BF16)
    od_w_in_b, od_w_out_b = od_w_in.astype(BF16), od_w_out.astype(BF16)

    for i in range(DEPTH):
        final = i == DEPTH - 1
        n_rows = n_lat if final else t
        tm = OUT_TM
        lat_rows = lambda r: (jnp.minimum(r, n_lat // tm - 1), 0)
        first = lambda r: (0, 0)
        if i % 2 == 0:
            e = i // 2
            lam_init = 0.8 - 0.6 * math.exp(-0.3 * i)
            p, vt = _proj(xs, mod[i], norm_g[i], ev_w_in_b, e, *rope_a, qk_norm[0], n_lat=n_lat,
                          even=True)
            ya = _diff_attention(p, vt, diff_lambda[e], diff_subln[e], lam_init=lam_init,
                                 q_row0=0, n_q=n_lat, k_row0=0, n_k=t)
            ya_ctx = ya if final else _diff_attention(
                p, vt, diff_lambda[e], diff_subln[e], lam_init=lam_init, q_row0=n_lat,
                n_q=CTX_LEN, k_row0=n_lat, n_k=CTX_LEN)
            o_f, o_b = _retention(p, ret_decay[e], n_lat=n_lat)
            bw = B_WIDTH
            args = [ya, ya_ctx, o_f, o_b, p, ret_gn[e].reshape(1, bw)]
            specs = [pl.BlockSpec((tm, A_WIDTH), lat_rows), pl.BlockSpec((tm, A_WIDTH), first),
                     pl.BlockSpec((tm, bw), lambda r: (r, 0)),
                     pl.BlockSpec((tm, bw), lambda r: (r, 0)),
                     pl.BlockSpec((tm, bw), lambda r: (r, (EV_IN - bw) // bw)),
                     pl.BlockSpec((1, bw), lambda r: (0, 0))]
            xs = _out_proj(xs, args, specs, ev_w_out_b, e, mod[i], final_norm, n_lat=n_lat,
                           n_rows=n_rows, even=True, final=final)
        else:
            o = i // 2
            p, vt = _proj(xs, mod[i], norm_g[i], od_w_in_b, o, *rope_c, qk_norm[o], n_lat=n_lat,
                          even=False)
            y = _gqa_attention(p, vt, q_row0=0, n_q=n_lat, k_row0=0, n_k=t)
            y_ctx = y if final else _gqa_attention(p, vt, q_row0=n_lat, n_q=CTX_LEN,
                                                   k_row0=n_lat, n_k=CTX_LEN)
            specs = [pl.BlockSpec((tm, d), lat_rows), pl.BlockSpec((tm, d), first)]
            xs = _out_proj(xs, [y, y_ctx], specs, od_w_out_b, o, mod[i], final_norm, n_lat=n_lat,
                           n_rows=n_rows, even=False, final=final)
    return xs[None]
```
